```python
import math
import jax, jax.numpy as jnp
from jax import lax
import numpy as np

D_MODEL = 1024
BATCH = 32
SEQ = 2048
DEPTH = 1

HEAD_DIM = 64
NSA_HEADS = D_MODEL // 2 // HEAD_DIM
NSA_KV_HEADS = 2
NSA_GROUP = NSA_HEADS // NSA_KV_HEADS
CMP_LEN = 32
CMP_STRIDE = 16
CMP_HIDDEN = 256
SLC_LEN = 64
SLC_TOPK = 6
SLC_Q_CHUNK = 32
WIN = 256
DIFF_QK_DIM = 32
DIFF_V_DIM = 64
DIFF_HEADS = D_MODEL // 2 // DIFF_V_DIM
ROPE_THETA = 500000.0
Q_BLOCK = 128
IN_COLS = (NSA_HEADS * HEAD_DIM + 6 * NSA_KV_HEADS * HEAD_DIM + 3 * NSA_HEADS
           + 4 * DIFF_HEADS * DIFF_QK_DIM + DIFF_HEADS * DIFF_V_DIM)
N_EXPERT_GROUPS = 4
EXPERTS_PER_GROUP = 8
N_EXPERTS = N_EXPERT_GROUPS * EXPERTS_PER_GROUP
EXPERT_TOPK = 2
EXPERT_HIDDEN = D_MODEL // 2
MOE_BLOCK = 512
EPS = 1e-6
NEG = -1e30
FORCE_BONUS = 1e4

kernel_name = "hymba_nsa_diffattn_hmoe_layer"


def rms_norm(x, g):
    x32 = x.astype(jnp.float32)
    y = x32 * lax.rsqrt(jnp.mean(x32 * x32, axis=-1, keepdims=True) + EPS)
    return (y * g.astype(jnp.float32)).astype(x.dtype)


def rope_tables(seq, rot_dim):
    inv = ROPE_THETA ** (-jnp.arange(0, rot_dim, 2, dtype=jnp.float32) / rot_dim)
    ang = jnp.arange(seq, dtype=jnp.float32)[:, None] * inv[None, :]
    return jnp.cos(ang), jnp.sin(ang)


def apply_rope(x, cos, sin):
    half = cos.shape[-1]
    x1, x2, xp = x[..., :half], x[..., half:2 * half], x[..., 2 * half:]
    c, s = cos.astype(x.dtype), sin.astype(x.dtype)
    return jnp.concatenate([x1 * c - x2 * s, x1 * s + x2 * c, xp], axis=-1)


def heads(t, n, d):
    b, s, _ = t.shape
    return t.reshape(b, s, n, d).transpose(0, 2, 1, 3)


def compress_blocks(t, pe, w1, w2):
    B, G, S, Dh = t.shape
    n_cmp = (S - CMP_LEN) // CMP_STRIDE + 1
    idx = jnp.arange(n_cmp)[:, None] * CMP_STRIDE + jnp.arange(CMP_LEN)[None, :]
    blocks = (t[:, :, idx, :] + pe).reshape(B, G, n_cmp, CMP_LEN * Dh)
    return jax.nn.gelu(blocks @ w1) @ w2


def selected_attention(q, k, v, sel, scale):
    B, G, NG, S, Dh = q.shape
    K = sel.shape[-1]
    n_chunk = S // SLC_Q_CHUNK
    kb = k.reshape(B, G, S // SLC_LEN, SLC_LEN, Dh)
    vb = v.reshape(B, G, S // SLC_LEN, SLC_LEN, Dh)
    gather = jax.vmap(jax.vmap(lambda blocks, idx: blocks[idx]))
    q_c = jnp.moveaxis(q.reshape(B, G, NG, n_chunk, SLC_Q_CHUNK, Dh), 3, 0)
    sel_c = jnp.moveaxis(sel.reshape(B, G, n_chunk, SLC_Q_CHUNK, K), 2, 0)
    t_c = jnp.arange(S).reshape(n_chunk, SLC_Q_CHUNK)
    offs = jnp.arange(SLC_LEN)

    def chunk(args):
        qc, ic, tc = args
        kg = gather(kb, ic).reshape(B, G, SLC_Q_CHUNK, K * SLC_LEN, Dh)
        vg = gather(vb, ic).reshape(B, G, SLC_Q_CHUNK, K * SLC_LEN, Dh)
        kpos = (ic[..., None] * SLC_LEN + offs).reshape(B, G, SLC_Q_CHUNK, K * SLC_LEN)
        mask = kpos <= tc[:, None]
        s = jnp.einsum('bgntd,bgtkd->bgntk', qc, kg).astype(jnp.float32) * scale
        p = jax.nn.softmax(jnp.where(mask[:, :, None], s, NEG), axis=-1)
        return jnp.einsum('bgntk,bgtkd->bgntd', p.astype(vg.dtype), vg)

    o = lax.map(chunk, (q_c, sel_c, t_c))
    return jnp.moveaxis(o, 0, 3).reshape(B, G, NG, S, Dh)


def window_attention(q, k, v, scale):
    B, G, NG, S, Dh = q.shape
    nq = S // Q_BLOCK
    nprev = WIN // Q_BLOCK
    pad = nprev * Q_BLOCK

    def band(t):
        tp = jnp.pad(t, ((0, 0), (0, 0), (pad, 0), (0, 0))).reshape(B, G, nq + nprev, Q_BLOCK, Dh)
        return jnp.concatenate([tp[:, :, i:i + nq] for i in range(nprev + 1)], axis=3)

    kb, vb = band(k), band(v)
    qb = q.reshape(B, G, NG, nq, Q_BLOCK, Dh)
    tq = jnp.arange(nq)[:, None] * Q_BLOCK + jnp.arange(Q_BLOCK)[None, :]
    tk = jnp.arange(nq)[:, None] * Q_BLOCK - pad + jnp.arange((nprev + 1) * Q_BLOCK)[None, :]
    tq3, tk3 = tq[:, :, None], tk[:, None, :]
    mask = (tk3 <= tq3) & (tk3 > tq3 - WIN) & (tk3 >= 0)
    s = jnp.einsum('bgnqid,bgqjd->bgnqij', qb, kb).astype(jnp.float32) * scale
    p = jax.nn.softmax(jnp.where(mask, s, NEG), axis=-1)
    o = jnp.einsum('bgnqij,bgqjd->bgnqid', p.astype(vb.dtype), vb)
    return o.reshape(B, G, NG, S, Dh)


def nsa_mixer(q, kv, gate_logits, pe_cmp, w_cmp1, w_cmp2, g_q, g_k, cos, sin):
    B, S, _ = q.shape
    dt = q.dtype
    scale = HEAD_DIM ** -0.5
    qh = rms_norm(heads(q, NSA_HEADS, HEAD_DIM), g_q)
    q_nope = qh.reshape(B, NSA_KV_HEADS, NSA_GROUP, S, HEAD_DIM)
    q_rope = apply_rope(qh, cos, sin).reshape(B, NSA_KV_HEADS, NSA_GROUP, S, HEAD_DIM)
    kc, vc, ks, vs, kw, vw = [heads(t, NSA_KV_HEADS, HEAD_DIM) for t in jnp.split(kv, 6, axis=-1)]
    t_pos = jnp.arange(S)

    k_cmp = rms_norm(compress_blocks(kc, pe_cmp[0], w_cmp1[0], w_cmp2[0]), g_k[0])
    v_cmp = compress_blocks(vc, pe_cmp[1], w_cmp1[1], w_cmp2[1])
    n_cmp = k_cmp.shape[2]
    cmp_end = jnp.arange(n_cmp) * CMP_STRIDE + CMP_LEN - 1
    cmp_mask = cmp_end[None, :] <= t_pos[:, None]
    cmp_any = jnp.any(cmp_mask, axis=-1)[:, None].astype(jnp.float32)
    s_cmp = jnp.einsum('bgnsd,bgcd->bgnsc', q_nope, k_cmp).astype(jnp.float32) * scale
    p_cmp = jax.nn.softmax(jnp.where(cmp_mask, s_cmp, NEG), axis=-1) * cmp_any
    o_cmp = jnp.einsum('bgnsc,bgcd->bgnsd', p_cmp.astype(dt), v_cmp)

    n_slc = S // SLC_LEN
    c_start = jnp.arange(n_cmp) * CMP_STRIDE
    s_start = jnp.arange(n_slc) * SLC_LEN
    overlap = jnp.clip(jnp.minimum(c_start[:, None] + CMP_LEN, s_start[None, :] + SLC_LEN)
                       - jnp.maximum(c_start[:, None], s_start[None, :]), 0, None).astype(jnp.float32) / CMP_LEN
    imp = jnp.einsum('bgnsc,cj->bgsj', p_cmp, overlap)
    blk = jnp.arange(n_slc)[None, :]
    cur = (t_pos // SLC_LEN)[:, None]
    valid = blk <= cur
    forced = (blk == 0) | (blk == cur) | (blk == cur - 1)
    score = jnp.where(valid, imp + jnp.where(forced, FORCE_BONUS, 0.0), NEG)
    _, sel = lax.top_k(score, min(SLC_TOPK, n_slc))
    k_s = apply_rope(rms_norm(ks, g_k[1]), cos, sin)
    o_slc = selected_attention(q_rope, k_s, vs, sel, scale)

    k_w = apply_rope(rms_norm(kw, g_k[2]), cos, sin)
    o_win = window_attention(q_rope, k_w, vw, scale)

    g = jax.nn.sigmoid(gate_logits.astype(jnp.float32)).astype(dt)
    g = g.reshape(B, S, NSA_HEADS, 3).transpose(0, 2, 1, 3).reshape(B, NSA_KV_HEADS, NSA_GROUP, S, 3)
    o = g[..., 0:1] * o_cmp + g[..., 1:2] * o_slc + g[..., 2:3] * o_win
    return o.reshape(B, NSA_HEADS, S, HEAD_DIM).transpose(0, 2, 1, 3).reshape(B, S, NSA_HEADS * HEAD_DIM)


def diff_mixer(q, k, v, lam_q1, lam_k1, lam_q2, lam_k2, g_q, g_k, g_o, lambda_init, cos, sin):
    B, S, _ = q.shape

    def qk_heads(t, g):
        t = t.reshape(B, S, DIFF_HEADS, 2, DIFF_QK_DIM).transpose(0, 2, 3, 1, 4)
        return apply_rope(rms_norm(t, g), cos, sin)

    qh, kh = qk_heads(q, g_q), qk_heads(k, g_k)
    vh = heads(v, DIFF_HEADS, DIFF_V_DIM)
    f32 = jnp.float32
    lam = (jnp.exp(jnp.sum(lam_q1.astype(f32) * lam_k1.astype(f32)))
           - jnp.exp(jnp.sum(lam_q2.astype(f32) * lam_k2.astype(f32))) + lambda_init)
    scale = DIFF_QK_DIM ** -0.5
    outs = []
    for i in range(S // Q_BLOCK):
        lo, hi = i * Q_BLOCK, (i + 1) * Q_BLOCK
        s = jnp.einsum('bhmqd,bhmkd->bhmqk', qh[:, :, :, lo:hi], kh[:, :, :, :hi]).astype(f32) * scale
        mask = jnp.arange(hi)[None, :] <= jnp.arange(lo, hi)[:, None]
        p = jax.nn.softmax(jnp.where(mask, s, NEG), axis=-1)
        a = p[:, :, 0] - lam * p[:, :, 1]
        outs.append(jnp.einsum('bhqk,bhkd->bhqd', a.astype(vh.dtype), vh[:, :, :hi]))
    o = jnp.concatenate(outs, axis=2)
    o = rms_norm(o, g_o) * (1.0 - lambda_init)
    return o.transpose(0, 2, 1, 3).reshape(B, S, DIFF_HEADS * DIFF_V_DIM)


def hier_moe(h, w_rg, b_rg, w_re, b_re, w_g, w_u, w_d):
    T, D = h.shape
    f32 = jnp.float32
    lg = (h @ w_rg).astype(f32) + b_rg.astype(f32)
    pg = jax.nn.softmax(lg, axis=-1)
    _, grp = lax.top_k(lg, 1)
    p_grp = jnp.take_along_axis(pg, grp, axis=-1)
    le = ((h @ w_re).astype(f32) + b_re.astype(f32)).reshape(T, N_EXPERT_GROUPS, EXPERTS_PER_GROUP)
    le_sel = jnp.take_along_axis(le, grp[:, :, None], axis=1)[:, 0]
    top_v, top_i = lax.top_k(le_sel, EXPERT_TOPK)
    weight = p_grp * jax.nn.softmax(top_v, axis=-1)
    expert = grp * EXPERTS_PER_GROUP + top_i

    e_flat = expert.reshape(-1)
    tok = jnp.repeat(jnp.arange(T), EXPERT_TOPK)
    w_flat = weight.reshape(-1)
    order = jnp.argsort(e_flat)
    e_s, tok_s, w_s = e_flat[order], tok[order], w_flat[order]
    counts = jnp.bincount(e_flat, length=N_EXPERTS)
    starts = jnp.cumsum(counts) - counts
    padded = (counts + MOE_BLOCK - 1) // MOE_BLOCK * MOE_BLOCK
    pend = jnp.cumsum(padded)
    pstarts = pend - padded
    dest = pstarts[e_s] + (jnp.arange(T * EXPERT_TOPK) - starts[e_s])
    n_blk = (T * EXPERT_TOPK + MOE_BLOCK - 1) // MOE_BLOCK + N_EXPERTS
    rows = n_blk * MOE_BLOCK
    buf = jnp.zeros((rows, D), h.dtype).at[dest].set(h[tok_s])
    blk_e = jnp.minimum(jnp.searchsorted(pend, jnp.arange(n_blk) * MOE_BLOCK, side='right'), N_EXPERTS - 1)

    def expert_block(args):
        xb, e = args
        return (jax.nn.silu(xb @ w_g[e]) * (xb @ w_u[e])) @ w_d[e]

    yb = lax.map(expert_block, (buf.reshape(n_blk, MOE_BLOCK, D), blk_e)).reshape(rows, D)
    return jnp.zeros((T, D), h.dtype).at[tok_s].add(yb[dest] * w_s[:, None].astype(h.dtype))


def setup_inputs(seed: int = 0) -> dict:
    key = jax.random.key(seed)
    ks = jax.random.split(key, 32)
    f32 = jnp.float32

    def nrm(k, shape, scale):
        return jax.random.normal(k, shape, f32) * scale

    D, L = D_MODEL, DEPTH
    return {
        'x': nrm(ks[0], (BATCH, SEQ, D), 1.0),
        'c': nrm(ks[1], (BATCH, D), 1.0),
        'w_ada': nrm(ks[2], (L, D, 6 * D), 0.3 * D ** -0.5),
        'b_ada': nrm(ks[3], (L, 6 * D), 0.01),
        'g_norm_mix': 1.0 + nrm(ks[4], (L, D), 0.05),
        'g_norm_ffn': 1.0 + nrm(ks[5], (L, D), 0.05),
        'w_in': nrm(ks[6], (L, D, IN_COLS), D ** -0.5),
        'g_nsa_q': 1.0 + nrm(ks[7], (L, HEAD_DIM), 0.05),
        'g_nsa_k': 1.0 + nrm(ks[8], (L, 3, HEAD_DIM), 0.05),
        'pe_cmp': nrm(ks[9], (L, 2, CMP_LEN, HEAD_DIM), 0.1),
        'w_cmp1': nrm(ks[10], (L, 2, CMP_LEN * HEAD_DIM, CMP_HIDDEN), (CMP_LEN * HEAD_DIM) ** -0.5),
        'w_cmp2': nrm(ks[11], (L, 2, CMP_HIDDEN, HEAD_DIM), CMP_HIDDEN ** -0.5),
        'g_diff_q': 1.0 + nrm(ks[12], (L, DIFF_QK_DIM), 0.05),
        'g_diff_k': 1.0 + nrm(ks[13], (L, DIFF_QK_DIM), 0.05),
        'lam_q1': nrm(ks[14], (L, DIFF_QK_DIM), 0.1),
        'lam_k1': nrm(ks[15], (L, DIFF_QK_DIM), 0.1),
        'lam_q2': nrm(ks[16], (L, DIFF_QK_DIM), 0.1),
        'lam_k2': nrm(ks[17], (L, DIFF_QK_DIM), 0.1),
        'g_diff_out': 1.0 + nrm(ks[18], (L, DIFF_V_DIM), 0.05),
        'w_out': nrm(ks[19], (L, D, D), D ** -0.5),
        'w_router_group': nrm(ks[20], (L, D, N_EXPERT_GROUPS), D ** -0.5),
        'b_router_group': nrm(ks[21], (L, N_EXPERT_GROUPS), 0.01),
        'w_router_expert': nrm(ks[22], (L, D, N_EXPERTS), D ** -0.5),
        'b_router_expert': nrm(ks[23], (L, N_EXPERTS), 0.01),
        'w_exp_gate': nrm(ks[24], (L, N_EXPERTS, D, EXPERT_HIDDEN), D ** -0.5),
        'w_exp_up': nrm(ks[25], (L, N_EXPERTS, D, EXPERT_HIDDEN), D ** -0.5),
        'w_exp_down': nrm(ks[26], (L, N_EXPERTS, EXPERT_HIDDEN, D), EXPERT_HIDDEN ** -0.5),
    }


def reference(x, c, w_ada, b_ada, g_norm_mix, g_norm_ffn, w_in, g_nsa_q, g_nsa_k, pe_cmp,
              w_cmp1, w_cmp2, g_diff_q, g_diff_k, lam_q1, lam_k1, lam_q2, lam_k2, g_diff_out,
              w_out, w_router_group, b_router_group, w_router_expert, b_router_expert,
              w_exp_gate, w_exp_up, w_exp_down):
    B, S, D = x.shape
    cos_n, sin_n = rope_tables(S, HEAD_DIM // 4)
    cos_d, sin_d = rope_tables(S, DIFF_QK_DIM // 4)
    sizes = (NSA_HEADS * HEAD_DIM, 6 * NSA_KV_HEADS * HEAD_DIM, 3 * NSA_HEADS,
             2 * DIFF_HEADS * DIFF_QK_DIM, 2 * DIFF_HEADS * DIFF_QK_DIM, DIFF_HEADS * DIFF_V_DIM)
    cuts = [int(v) for v in np.cumsum(sizes)[:-1]]
    for l in range(DEPTH):
        lambda_init = 0.8 - 0.6 * math.exp(-0.3 * l)
        mod = (c @ w_ada[l] + b_ada[l])[:, None, :]
        shift1, scale1, gate1, shift2, scale2, gate2 = jnp.split(mod, 6, axis=-1)

        h = rms_norm(x, g_norm_mix[l]) * (1.0 + scale1) + shift1
        proj = h @ w_in[l]
        q_n, kv_n, gate_n, q_d, k_d, v_d = jnp.split(proj, cuts, axis=-1)
        o_nsa = nsa_mixer(q_n, kv_n, gate_n, pe_cmp[l], w_cmp1[l], w_cmp2[l],
                          g_nsa_q[l], g_nsa_k[l], cos_n, sin_n)
        o_diff = diff_mixer(q_d, k_d, v_d, lam_q1[l], lam_k1[l], lam_q2[l], lam_k2[l],
                            g_diff_q[l], g_diff_k[l], g_diff_out[l], lambda_init, cos_d, sin_d)
        x = x + gate1 * (jnp.concatenate([o_nsa, o_diff], axis=-1) @ w_out[l])

        h = rms_norm(x, g_norm_ffn[l]) * (1.0 + scale2) + shift2
        y = hier_moe(h.reshape(B * S, D), w_router_group[l], b_router_group[l],
                     w_router_expert[l], b_router_expert[l],
                     w_exp_gate[l], w_exp_up[l], w_exp_down[l])
        x = x + gate2 * y.reshape(B, S, D)
    return x
```

```python
import functools
import math

import numpy as np
import jax
import jax.numpy as jnp
from jax import lax
from jax.experimental import pallas as pl
from jax.experimental.pallas import tpu as pltpu

F32, BF16, I32 = jnp.float32, jnp.bfloat16, jnp.int32
HIGHEST = lax.Precision.HIGHEST

HEAD_DIM = 64
NSA_HEADS = 8
NSA_KV_HEADS = 2
NSA_GROUP = NSA_HEADS // NSA_KV_HEADS
CMP_LEN = 32
CMP_STRIDE = 16
CMP_HIDDEN = 256
SLC_LEN = 64
SLC_TOPK = 6
WIN = 256
DIFF_QK_DIM = 32
DIFF_V_DIM = 64
DIFF_HEADS = 8
ROPE_THETA = 500000.0
N_EXPERT_GROUPS = 4
EXPERTS_PER_GROUP = 8
N_EXPERTS = N_EXPERT_GROUPS * EXPERTS_PER_GROUP
EPS = 1e-6
NEG = -1e30
FORCE_BONUS = 1e4
LAMBDA_INIT = 0.8 - 0.6 * math.exp(-0.3 * 0)

LANE = 128
SUBLANE = 8
VMEM_LIMIT = 48 * 1024 * 1024

TM_PROJ = 512
TQ_ATT = 128
TK_ATT = 512
MOE_BLOCK = 512
DMA_ROWS = SUBLANE * LANE
SENTINEL = -3e38


def _dot(a, b, **kw):
    return jnp.dot(a, b, preferred_element_type=F32, **kw)


def _dot_nt(a, b):
    return lax.dot_general(a, b, (((1,), (1,)), ((), ())), preferred_element_type=F32)


def _params(*sem):
    return pltpu.CompilerParams(dimension_semantics=sem, vmem_limit_bytes=VMEM_LIMIT)


def _ada_kernel(c_ref, w_ref, b_ref, o_ref):
    o_ref[...] = _dot(c_ref[...], w_ref[...], precision=HIGHEST) + b_ref[...]


def _ada(c, w, b):
    B, D = c.shape
    n = w.shape[1]
    return pl.pallas_call(
        _ada_kernel,
        grid=(n // D,),
        in_specs=[pl.BlockSpec((B, D), lambda j: (0, 0)),
                  pl.BlockSpec((D, D), lambda j: (0, j)),
                  pl.BlockSpec((1, D), lambda j: (0, j))],
        out_specs=pl.BlockSpec((B, D), lambda j: (0, j)),
        out_shape=jax.ShapeDtypeStruct((B, n), F32),
        compiler_params=_params("arbitrary"),
    )(c, w, b.reshape(1, n))


_C_Q, _C_KCV, _C_KS, _C_KW, _C_DQ, _C_DK, _C_DV, _C_GATE, _C_END = (
    0, 512, 768, 1024, 1280, 1792, 2304, 2816, 2944)


def _seg_rms(y, bd, inv_n):
    y2 = y * y
    hi = y2.astype(BF16)
    lo = (y2 - hi.astype(F32)).astype(BF16)
    ss = _dot(hi, bd) + _dot(lo, bd)
    return y * lax.rsqrt(ss * inv_n + EPS)


def _rope(y, c, sa, sb, half):
    return y * c + pltpu.roll(y, LANE - half, 1) * sa + pltpu.roll(y, half, 1) * sb


def _inproj_kernel(x_ref, mod_ref, gmix_ref, w_ref, gains_ref, bd_ref, rope_ref,
                   qn_ref, qr_ref, kcv_ref, kv_ref, gate_ref, dq_ref, dkv_ref):
    x = x_ref[0]
    shift, scale = mod_ref[0, 0:1, :], mod_ref[0, 1:2, :]
    ms = jnp.mean(x * x, axis=-1, keepdims=True)
    h = x * lax.rsqrt(ms + EPS) * gmix_ref[...] * (1.0 + scale) + shift
    hb = h.astype(BF16)
    bd64, bd32 = bd_ref[0], bd_ref[1]
    cn, san, sbn = rope_ref[0], rope_ref[1], rope_ref[2]
    cd, sad, sbd = rope_ref[3], rope_ref[4], rope_ref[5]
    half_n, half_d = HEAD_DIM // 8, DIFF_QK_DIM // 8

    def mm(c0):
        y = _dot(hb, w_ref[:, c0:c0 + 2 * LANE])
        return y[:, :LANE], y[:, LANE:]

    for c in range(2):
        for j, y in enumerate(mm(_C_Q + c * 2 * LANE)):
            col = (2 * c + j) * LANE
            yn = _seg_rms(y, bd64, 1.0 / HEAD_DIM) * gains_ref[0:1, :]
            qn_ref[0, :, col:col + LANE] = yn.astype(BF16)
            qr_ref[0, :, col:col + LANE] = _rope(yn, cn, san, sbn, half_n).astype(BF16)

    kc, vc = mm(_C_KCV)
    kcv_ref[0, :, 0:LANE] = kc.astype(BF16)
    kcv_ref[0, :, LANE:2 * LANE] = vc.astype(BF16)
    for i, c0 in enumerate((_C_KS, _C_KW)):
        k, v = mm(c0)
        kn = _seg_rms(k, bd64, 1.0 / HEAD_DIM) * gains_ref[1 + i:2 + i, :]
        kv_ref[0, :, 2 * i * LANE:(2 * i + 1) * LANE] = _rope(kn, cn, san, sbn, half_n).astype(BF16)
        kv_ref[0, :, (2 * i + 1) * LANE:(2 * i + 2) * LANE] = v.astype(BF16)

    for i, (c0, out_ref) in enumerate(((_C_DQ, dq_ref), (_C_DK, dkv_ref))):
        for c in range(2):
            for j, y in enumerate(mm(c0 + c * 2 * LANE)):
                col = (2 * c + j) * LANE
                yn = _seg_rms(y, bd32, 1.0 / DIFF_QK_DIM) * gains_ref[3 + i:4 + i, :]
                out_ref[0, :, col:col + LANE] = _rope(yn, cd, sad, sbd, half_d).astype(BF16)
    for c in range(2):
        for j, y in enumerate(mm(_C_DV + c * 2 * LANE)):
            col = 4 * LANE + (2 * c + j) * LANE
            dkv_ref[0, :, col:col + LANE] = y.astype(BF16)

    g = _dot(hb, w_ref[:, _C_GATE:_C_END])
    gate_ref[0] = 1.0 / (1.0 + jnp.exp(-g))


def _rope_tables(seq, head_dim):
    rot = head_dim // 4
    half = rot // 2
    inv = ROPE_THETA ** (-jnp.arange(0, rot, 2, dtype=F32) / rot)
    ang = jnp.arange(seq, dtype=F32)[:, None] * inv[None, :]
    cos, sin = jnp.cos(ang), jnp.sin(ang)
    pad = jnp.zeros((seq, head_dim - 2 * half), F32)
    c = jnp.concatenate([cos, cos, pad + 1.0], axis=-1)
    sa = jnp.concatenate([-sin, jnp.zeros_like(sin), pad], axis=-1)
    sb = jnp.concatenate([jnp.zeros_like(sin), sin, pad], axis=-1)
    reps = LANE // head_dim
    return jnp.stack([jnp.tile(t, (1, reps)) for t in (c, sa, sb)])


def _block_diag_ones(seg):
    i = np.arange(LANE)
    return jnp.asarray((i[:, None] // seg) == (i[None, :] // seg), dtype=BF16)


def _inproj(x, mod3, g_mix, w_in, g_nsa_q, g_nsa_k, g_diff_q, g_diff_k):
    B, S, D = x.shape
    tm = TM_PROJ
    gate_w = w_in[:, 1280:1304]
    w = jnp.concatenate(
        [w_in[:, :1280], w_in[:, 1304:], jnp.pad(gate_w, ((0, 0), (0, LANE - gate_w.shape[1])))],
        axis=1).astype(BF16)
    scale = HEAD_DIM ** -0.5
    gains = jnp.stack([
        jnp.tile(g_nsa_q, LANE // HEAD_DIM) * scale,
        jnp.tile(g_nsa_k[1], LANE // HEAD_DIM), jnp.tile(g_nsa_k[2], LANE // HEAD_DIM),
        jnp.tile(g_diff_q, LANE // DIFF_QK_DIM), jnp.tile(g_diff_k, LANE // DIFF_QK_DIM),
        jnp.zeros(LANE), jnp.zeros(LANE), jnp.zeros(LANE)]).astype(F32)
    bd = jnp.stack([_block_diag_ones(HEAD_DIM), _block_diag_ones(DIFF_QK_DIM)])
    rope = jnp.concatenate([_rope_tables(S, HEAD_DIM), _rope_tables(S, DIFF_QK_DIM)])

    def tok(width):
        return pl.BlockSpec((1, tm, width), lambda s, b: (b, s, 0))

    def out(width, dt=BF16):
        return jax.ShapeDtypeStruct((B, S, width), dt)

    return pl.pallas_call(
        _inproj_kernel,
        grid=(S // tm, B),
        in_specs=[tok(D),
                  pl.BlockSpec((1, 6, D), lambda s, b: (b, 0, 0)),
                  pl.BlockSpec((1, D), lambda s, b: (0, 0)),
                  pl.BlockSpec((D, _C_END), lambda s, b: (0, 0)),
                  pl.BlockSpec((SUBLANE, LANE), lambda s, b: (0, 0)),
                  pl.BlockSpec((2, LANE, LANE), lambda s, b: (0, 0, 0)),
                  pl.BlockSpec((6, tm, LANE), lambda s, b: (0, s, 0))],
        out_specs=[tok(512), tok(512), tok(256), tok(512), tok(LANE), tok(512), tok(1024)],
        out_shape=[out(512), out(512), out(256), out(512), out(LANE, F32), out(512), out(1024)],
        compiler_params=_params("arbitrary", "arbitrary"),
    )(x, mod3, g_mix.reshape(1, D), w, gains, bd, rope)


def _gelu_tanh(x):
    return 0.5 * x * (1.0 + jnp.tanh(math.sqrt(2.0 / math.pi) * (x + 0.044715 * (x * x * x))))


def _compress_kernel(r_ref, w1_ref, pe_ref, w2_ref, gk_ref, kc_ref, vc_ref):
    half = CMP_STRIDE * HEAD_DIM
    n_rows = r_ref.shape[2]
    for kv, out_ref in enumerate((kc_ref, vc_ref)):
        r = r_ref[0, kv]
        a = _dot(r, w1_ref[kv, 0:half, :])
        b = _dot(r, w1_ref[kv, half:2 * half, :])
        pb = _dot(pe_ref[kv], w1_ref[kv])
        hid = a + pltpu.roll(b, n_rows - 1, 0) + (pb[0:1] + pb[1:2])
        o = _dot(_gelu_tanh(hid).astype(BF16), w2_ref[kv])
        if kv == 0:
            o = o * lax.rsqrt(jnp.mean(o * o, axis=-1, keepdims=True) + EPS) * gk_ref[...]
        for g in range(NSA_KV_HEADS):
            out_ref[0, g] = o[g * LANE:(g + 1) * LANE].astype(BF16)


def _compress(kcv, pe_cmp, w_cmp1, w_cmp2, g_k0):
    B, S, _ = kcv.shape
    n_rows = S // CMP_STRIDE
    width = CMP_STRIDE * HEAD_DIM
    r = kcv.reshape(B, n_rows, CMP_STRIDE, 2, NSA_KV_HEADS, HEAD_DIM)
    r = r.transpose(0, 3, 4, 1, 2, 5).reshape(B, 2, NSA_KV_HEADS * n_rows, width)
    pe = pe_cmp.reshape(2, 1, CMP_LEN * HEAD_DIM)
    pe_hi = pe.astype(BF16)
    pe_lo = (pe - pe_hi.astype(F32)).astype(BF16)
    pe2 = jnp.concatenate([pe_hi, pe_lo, jnp.zeros((2, SUBLANE - 2, CMP_LEN * HEAD_DIM), BF16)], axis=1)
    shape = jax.ShapeDtypeStruct((B, NSA_KV_HEADS, n_rows, HEAD_DIM), BF16)
    spec = pl.BlockSpec((1, NSA_KV_HEADS, n_rows, HEAD_DIM), lambda b: (b, 0, 0, 0))
    return pl.pallas_call(
        _compress_kernel,
        grid=(B,),
        in_specs=[pl.BlockSpec((1, 2, NSA_KV_HEADS * n_rows, width), lambda b: (b, 0, 0, 0)),
                  pl.BlockSpec((2, 2 * width, CMP_HIDDEN), lambda b: (0, 0, 0)),
                  pl.BlockSpec((2, SUBLANE, 2 * width), lambda b: (0, 0, 0)),
                  pl.BlockSpec((2, CMP_HIDDEN, HEAD_DIM), lambda b: (0, 0, 0)),
                  pl.BlockSpec((1, HEAD_DIM), lambda b: (0, 0))],
        out_specs=[spec, spec],
        out_shape=[shape, shape],
        compiler_params=_params("arbitrary"),
    )(r, w_cmp1.astype(BF16), pe2, w_cmp2.astype(BF16), g_k0.reshape(1, HEAD_DIM))


def _softmax_rows(s):
    m = jnp.max(s, axis=-1, keepdims=True)
    e = jnp.exp(s - m)
    return e / jnp.sum(e, axis=-1, keepdims=True)


def _nsa_kernel(qn_ref, qr_ref, kc_ref, vc_ref, kv_ref, gate_ref, ovl_ref, exp_ref,
                o_ref, selbias_ref):
    tq, tk = TQ_ATT, TK_ATT
    qi = pl.program_id(1)
    t_col = qi * tq + lax.broadcasted_iota(I32, (tq, 1), 0)
    lane_i = lax.broadcasted_iota(I32, (1, LANE), 1)
    lane_f = lane_i.astype(F32)
    n_cmp = (kv_ref.shape[1] - CMP_LEN) // CMP_STRIDE + 1
    n_slc = kv_ref.shape[1] // SLC_LEN
    cmp_ok = (lane_i * CMP_STRIDE + (CMP_LEN - 1) <= t_col) & (lane_i < n_cmp)
    cmp_any = (t_col >= CMP_LEN - 1).astype(F32)
    cur = t_col // SLC_LEN
    blk_valid = (lane_i <= cur) & (lane_i < n_slc)
    forced = (lane_i == 0) | (lane_i == cur) | (lane_i == cur - 1)
    n_chunks = (qi * tq + tq + tk - 1) // tk
    w_start = pl.multiple_of(jnp.maximum(qi * tq - WIN, 0), tq)
    w_len = WIN + tq
    w_pos = w_start + lax.broadcasted_iota(I32, (1, w_len), 1)
    w_bias = jnp.where((w_pos <= t_col) & (w_pos > t_col - WIN), 0.0, NEG)
    w_bias4 = jnp.concatenate([w_bias] * NSA_GROUP, axis=0)

    for g in range(NSA_KV_HEADS):
        heads = [g * NSA_GROUP + n for n in range(NSA_GROUP)]
        kc, vc = kc_ref[0, g], vc_ref[0, g]
        psum = jnp.zeros((tq, LANE), F32)
        o_cmp = []
        for h in heads:
            s = _dot_nt(qn_ref[0, :, h * HEAD_DIM:(h + 1) * HEAD_DIM], kc)
            p = _softmax_rows(jnp.where(cmp_ok, s, NEG)) * cmp_any
            psum = psum + p
            o_cmp.append(_dot(p.astype(BF16), vc))
        imp = _dot(psum, ovl_ref[...], precision=HIGHEST)
        score = jnp.where(blk_valid, imp + jnp.where(forced, FORCE_BONUS, 0.0), SENTINEL)
        sel = jnp.zeros((tq, LANE), F32)
        for _ in range(SLC_TOPK):
            m = jnp.max(score, axis=-1, keepdims=True)
            first = jnp.min(jnp.where(score == m, lane_f, float(LANE)), axis=-1, keepdims=True)
            pick = (lane_f == first) & (m > SENTINEL)
            sel = jnp.where(pick, 1.0, sel)
            score = jnp.where(pick, SENTINEL, score)
        selbias_ref[...] = (_dot(sel.astype(BF16), exp_ref[...]) - 1.0) * (-NEG)

        q4 = jnp.concatenate([qr_ref[0, :, h * HEAD_DIM:(h + 1) * HEAD_DIM] for h in heads], axis=0)
        k_col = 0 * LANE + g * HEAD_DIM
        v_col = 1 * LANE + g * HEAD_DIM

        def slc_step(c, carry):
            m, l, acc = carry
            k0 = pl.multiple_of(c * tk, tk)
            k = kv_ref[0, pl.ds(k0, tk), k_col:k_col + HEAD_DIM]
            v = kv_ref[0, pl.ds(k0, tk), v_col:v_col + HEAD_DIM]
            kpos = k0 + lax.broadcasted_iota(I32, (1, tk), 1)
            bias = selbias_ref[:, pl.ds(k0, tk)] + jnp.where(kpos <= t_col, 0.0, NEG)
            s = _dot_nt(q4, k) + jnp.concatenate([bias] * NSA_GROUP, axis=0)
            m_new = jnp.maximum(m, jnp.max(s, axis=-1, keepdims=True))
            alpha = jnp.exp(m - m_new)
            p = jnp.exp(s - m_new)
            l = alpha * l + jnp.sum(p, axis=-1, keepdims=True)
            acc = alpha * acc + _dot(p.astype(BF16), v)
            return m_new, l, acc

        init = (jnp.full((NSA_GROUP * tq, 1), NEG, F32), jnp.zeros((NSA_GROUP * tq, 1), F32),
                jnp.zeros((NSA_GROUP * tq, HEAD_DIM), F32))
        _, l, acc = lax.fori_loop(0, n_chunks, slc_step, init)
        o_slc = acc / l

        kw = kv_ref[0, pl.ds(w_start, w_len), 2 * LANE + g * HEAD_DIM:2 * LANE + (g + 1) * HEAD_DIM]
        vw = kv_ref[0, pl.ds(w_start, w_len), 3 * LANE + g * HEAD_DIM:3 * LANE + (g + 1) * HEAD_DIM]
        pw = _softmax_rows(_dot_nt(q4, kw) + w_bias4)
        o_win = _dot(pw.astype(BF16), vw)

        for n, h in enumerate(heads):
            rows = slice(n * tq, (n + 1) * tq)
            o = (gate_ref[0, :, 3 * h:3 * h + 1] * o_cmp[n]
                 + gate_ref[0, :, 3 * h + 1:3 * h + 2] * o_slc[rows]
                 + gate_ref[0, :, 3 * h + 2:3 * h + 3] * o_win[rows])
            o_ref[0, :, h * HEAD_DIM:(h + 1) * HEAD_DIM] = o.astype(BF16)


def _nsa(qn, qr, kcmp, vcmp, kv, gates):
    B, S, W = qn.shape
    tq = TQ_ATT
    n_cmp = (S - CMP_LEN) // CMP_STRIDE + 1
    n_slc = S // SLC_LEN
    c0 = np.arange(LANE)[:, None] * CMP_STRIDE
    s0 = np.arange(LANE)[None, :] * SLC_LEN
    ovl = np.clip(np.minimum(c0 + CMP_LEN, s0 + SLC_LEN) - np.maximum(c0, s0), 0, None) / CMP_LEN
    ovl = ovl * (np.arange(LANE)[:, None] < n_cmp) * (np.arange(LANE)[None, :] < n_slc)
    expand = (np.arange(S)[None, :] // SLC_LEN) == np.arange(LANE)[:, None]
    cmp_spec = pl.BlockSpec((1,) + kcmp.shape[1:], lambda b, q: (b, 0, 0, 0))
    return pl.pallas_call(
        _nsa_kernel,
        grid=(B, S // tq),
        in_specs=[pl.BlockSpec((1, tq, W), lambda b, q: (b, q, 0)),
                  pl.BlockSpec((1, tq, W), lambda b, q: (b, q, 0)),
                  cmp_spec, cmp_spec,
                  pl.BlockSpec((1, S, kv.shape[2]), lambda b, q: (b, 0, 0)),
                  pl.BlockSpec((1, tq, LANE), lambda b, q: (b, q, 0)),
                  pl.BlockSpec((LANE, LANE), lambda b, q: (0, 0)),
                  pl.BlockSpec((LANE, S), lambda b, q: (0, 0))],
        out_specs=pl.BlockSpec((1, tq, W), lambda b, q: (b, q, 0)),
        out_shape=jax.ShapeDtypeStruct((B, S, W), BF16),
        scratch_shapes=[pltpu.VMEM((tq, S), F32)],
        compiler_params=_params("arbitrary", "arbitrary"),
    )(qn, qr, kcmp, vcmp, kv, gates, jnp.asarray(ovl, F32), jnp.asarray(expand, BF16))


def _diff_kernel(q_ref, kv_ref, lam_ref, go_ref, o_ref):
    tq, tk = TQ_ATT, TK_ATT
    qi = pl.program_id(1)
    t_col = qi * tq + lax.broadcasted_iota(I32, (tq, 1), 0)
    n_chunks = (qi * tq + tq + tk - 1) // tk
    scale = DIFF_QK_DIM ** -0.5
    lq = lam_ref[...]
    lam = (jnp.exp(jnp.sum(lq[0:1] * lq[1:2], axis=-1, keepdims=True))
           - jnp.exp(jnp.sum(lq[2:3] * lq[3:4], axis=-1, keepdims=True)) + LAMBDA_INIT)
    v_base = DIFF_HEADS * 2 * DIFF_QK_DIM

    for h in range(DIFF_HEADS):
        cols = [h * 2 * DIFF_QK_DIM + m * DIFF_QK_DIM for m in range(2)]
        q = [q_ref[0, :, c:c + DIFF_QK_DIM] for c in cols]

        def step(c, carry):
            k0 = pl.multiple_of(c * tk, tk)
            kpos = k0 + lax.broadcasted_iota(I32, (1, tk), 1)
            bias = jnp.where(kpos <= t_col, 0.0, NEG)
            v = kv_ref[0, pl.ds(k0, tk), v_base + h * DIFF_V_DIM:v_base + (h + 1) * DIFF_V_DIM]
            new = []
            for mp in range(2):
                m, l, acc = carry[mp]
                k = kv_ref[0, pl.ds(k0, tk), cols[mp]:cols[mp] + DIFF_QK_DIM]
                s = _dot_nt(q[mp], k) * scale + bias
                m_new = jnp.maximum(m, jnp.max(s, axis=-1, keepdims=True))
                alpha = jnp.exp(m - m_new)
                p = jnp.exp(s - m_new)
                l = alpha * l + jnp.sum(p, axis=-1, keepdims=True)
                acc = alpha * acc + _dot(p.astype(BF16), v)
                new.append((m_new, l, acc))
            return tuple(new)

        one = (jnp.full((tq, 1), NEG, F32), jnp.zeros((tq, 1), F32), jnp.zeros((tq, DIFF_V_DIM), F32))
        (_, l0, a0), (_, l1, a1) = lax.fori_loop(0, n_chunks, step, (one, one))
        o = a0 / l0 - lam * (a1 / l1)
        o = o * lax.rsqrt(jnp.mean(o * o, axis=-1, keepdims=True) + EPS) * go_ref[...] * (1.0 - LAMBDA_INIT)
        o_ref[0, :, h * DIFF_V_DIM:(h + 1) * DIFF_V_DIM] = o.astype(BF16)


def _diff(dq, dkv, lam4, g_out):
    B, S, W = dq.shape
    tq = TQ_ATT
    return pl.pallas_call(
        _diff_kernel,
        grid=(B, S // tq),
        in_specs=[pl.BlockSpec((1, tq, W), lambda b, q: (b, q, 0)),
                  pl.BlockSpec((1, S, dkv.shape[2]), lambda b, q: (b, 0, 0)),
                  pl.BlockSpec(lam4.shape, lambda b, q: (0, 0)),
                  pl.BlockSpec((1, DIFF_V_DIM), lambda b, q: (0, 0))],
        out_specs=pl.BlockSpec((1, tq, W), lambda b, q: (b, q, 0)),
        out_shape=jax.ShapeDtypeStruct((B, S, W), BF16),
        compiler_params=_params("arbitrary", "arbitrary"),
    )(dq, dkv, lam4, g_out.reshape(1, DIFF_V_DIM))


def _outproj_kernel(x_ref, on_ref, od_ref, wo_ref, mod_ref, gffn_ref, wr_ref, br_ref,
                    x1_ref, h_ref, route_ref):
    half = on_ref.shape[2]
    gate1 = mod_ref[0, 2:3, :]
    shift2, scale2 = mod_ref[0, 3:4, :], mod_ref[0, 4:5, :]
    attn = _dot(on_ref[0], wo_ref[0:half, :]) + _dot(od_ref[0], wo_ref[half:2 * half, :])
    x1 = x_ref[0] + gate1 * attn
    x1_ref[0] = x1
    ms = jnp.mean(x1 * x1, axis=-1, keepdims=True)
    h = x1 * lax.rsqrt(ms + EPS) * gffn_ref[...] * (1.0 + scale2) + shift2
    h_ref[0] = h

    r = _dot(h, wr_ref[...], precision=HIGHEST) + br_ref[...]
    lg, le = r[:, :LANE], r[:, LANE:]
    lane = lax.broadcasted_iota(I32, (1, LANE), 1).astype(F32)
    lg = jnp.where(lane < N_EXPERT_GROUPS, lg, SENTINEL)
    mg = jnp.max(lg, axis=-1, keepdims=True)
    p_grp = 1.0 / jnp.sum(jnp.exp(lg - mg), axis=-1, keepdims=True)
    grp = jnp.min(jnp.where(lg == mg, lane, float(LANE)), axis=-1, keepdims=True)
    lo = grp * EXPERTS_PER_GROUP
    le = jnp.where((lane >= lo) & (lane < lo + EXPERTS_PER_GROUP), le, SENTINEL)
    v1 = jnp.max(le, axis=-1, keepdims=True)
    i1 = jnp.min(jnp.where(le == v1, lane, float(LANE)), axis=-1, keepdims=True)
    le = jnp.where(lane == i1, SENTINEL, le)
    v2 = jnp.max(le, axis=-1, keepdims=True)
    i2 = jnp.min(jnp.where(le == v2, lane, float(LANE)), axis=-1, keepdims=True)
    e2 = jnp.exp(v2 - v1)
    w1 = p_grp / (1.0 + e2)
    w2 = p_grp * e2 / (1.0 + e2)
    route_ref[0] = jnp.where(lane == 0, i1, jnp.where(lane == 1, i2,
                             jnp.where(lane == 2, w1, jnp.where(lane == 3, w2, 0.0))))


def _outproj(x, o_nsa, o_diff, w_out, mod3, g_ffn, w_rg, b_rg, w_re, b_re):
    B, S, D = x.shape
    tm = TM_PROJ
    half = o_nsa.shape[2]
    wr = jnp.zeros((D, 2 * LANE), F32).at[:, :N_EXPERT_GROUPS].set(w_rg)
    wr = wr.at[:, LANE:LANE + N_EXPERTS].set(w_re)
    br = jnp.zeros((1, 2 * LANE), F32).at[0, :N_EXPERT_GROUPS].set(b_rg)
    br = br.at[0, LANE:LANE + N_EXPERTS].set(b_re)

    def tok(width):
        return pl.BlockSpec((1, tm, width), lambda b, s: (b, s, 0))

    return pl.pallas_call(
        _outproj_kernel,
        grid=(B, S // tm),
        in_specs=[tok(D), tok(half), tok(half),
                  pl.BlockSpec((D, D), lambda b, s: (0, 0)),
                  pl.BlockSpec((1, 6, D), lambda b, s: (b, 0, 0)),
                  pl.BlockSpec((1, D), lambda b, s: (0, 0)),
                  pl.BlockSpec((D, 2 * LANE), lambda b, s: (0, 0)),
                  pl.BlockSpec((1, 2 * LANE), lambda b, s: (0, 0))],
        out_specs=[tok(D), tok(D), tok(LANE)],
        out_shape=[jax.ShapeDtypeStruct((B, S, D), F32), jax.ShapeDtypeStruct((B, S, D), F32),
                   jax.ShapeDtypeStruct((B, S, LANE), F32)],
        compiler_params=_params("arbitrary", "arbitrary"),
    )(x, o_nsa, o_diff, w_out.astype(BF16), mod3, g_ffn.reshape(1, D), wr, br)


def _row_copy(src_hbm, dst, row, slot, sem):
    return pltpu.make_async_copy(src_hbm.at[pl.ds(row, 1)], dst.at[pl.ds(slot, 1)], sem)


def _issue_row_copies(idx_smem, src_hbm, dst, dst_base, sem):
    def body(r, carry):
        row = idx_smem[r // LANE, r % LANE]
        _row_copy(src_hbm, dst, row, dst_base + r, sem).start()
        return carry
    lax.fori_loop(0, DMA_ROWS, body, 0, unroll=8)


def _gather_kernel(idx_hbm, src_hbm, o_hbm, idx_smem, idx_sem, sem):
    i = pl.program_id(0)
    cp = pltpu.make_async_copy(idx_hbm.at[i], idx_smem, idx_sem)
    cp.start()
    cp.wait()
    base = i * DMA_ROWS
    _issue_row_copies(idx_smem, src_hbm, o_hbm, base, sem)
    pltpu.make_async_copy(src_hbm.at[pl.ds(0, DMA_ROWS)], o_hbm.at[pl.ds(base, DMA_ROWS)], sem).wait()


def _gather_rows(row_src, h):
    n_steps = row_src.shape[0]
    D = h.shape[1]
    return pl.pallas_call(
        _gather_kernel,
        grid=(n_steps,),
        in_specs=[pl.BlockSpec(memory_space=pl.ANY), pl.BlockSpec(memory_space=pl.ANY)],
        out_specs=pl.BlockSpec(memory_space=pl.ANY),
        out_shape=jax.ShapeDtypeStruct((n_steps * DMA_ROWS, D), h.dtype),
        scratch_shapes=[pltpu.SMEM((SUBLANE, LANE), I32), pltpu.SemaphoreType.DMA(()),
                        pltpu.SemaphoreType.DMA(())],
        compiler_params=_params("arbitrary"),
    )(row_src, h)


def _expert_kernel(blk_e_ref, n_used_ref, x_ref, wg_ref, wu_ref, wd_ref, o_ref):
    i = pl.program_id(0)

    @pl.when(i < n_used_ref[0])
    def _():
        xb = x_ref[...].astype(BF16)
        a = _dot(xb, wg_ref[0])
        u = _dot(xb, wu_ref[0])
        act = a * (1.0 / (1.0 + jnp.exp(-a))) * u
        o_ref[...] = _dot(act.astype(BF16), wd_ref[0])

    @pl.when(i >= n_used_ref[0])
    def _():
        o_ref[...] = jnp.zeros_like(o_ref)


def _experts(blk_e, n_used, buf, w_g, w_u, w_d):
    rows, D = buf.shape
    H = w_g.shape[2]
    bm = MOE_BLOCK
    grid_spec = pltpu.PrefetchScalarGridSpec(
        num_scalar_prefetch=2,
        grid=(rows // bm,),
        in_specs=[pl.BlockSpec((bm, D), lambda i, be, nu: (i, 0)),
                  pl.BlockSpec((1, D, H), lambda i, be, nu: (be[i], 0, 0)),
                  pl.BlockSpec((1, D, H), lambda i, be, nu: (be[i], 0, 0)),
                  pl.BlockSpec((1, H, D), lambda i, be, nu: (be[i], 0, 0))],
        out_specs=pl.BlockSpec((bm, D), lambda i, be, nu: (i, 0)))
    return pl.pallas_call(
        _expert_kernel,
        grid_spec=grid_spec,
        out_shape=jax.ShapeDtypeStruct((rows, D), F32),
        compiler_params=_params("arbitrary"),
    )(blk_e, n_used, buf, w_g.astype(BF16), w_u.astype(BF16), w_d.astype(BF16))


def _combine_kernel(pos_hbm, yb_hbm, x1_ref, route_ref, mod_ref, o_ref, idx_smem, rows_ref, idx_sem, sem):
    i = pl.program_id(0)
    cp = pltpu.make_async_copy(pos_hbm.at[i], idx_smem, idx_sem)
    cp.start()
    cp.wait()
    _issue_row_copies(idx_smem, yb_hbm, rows_ref, 0, sem)
    pltpu.make_async_copy(yb_hbm.at[pl.ds(0, DMA_ROWS)], rows_ref, sem).wait()
    tm = DMA_ROWS // 2
    gate2 = mod_ref[0, 5:6, :]
    y = rows_ref[0:tm, :] * route_ref[:, 2:3] + rows_ref[tm:2 * tm, :] * route_ref[:, 3:4]
    o_ref[...] = x1_ref[...] + gate2 * y


def _combine(pos, yb, x1, route, mod3, seq):
    T, D = x1.shape
    tm = DMA_ROWS // 2
    per_seq = seq // tm
    return pl.pallas_call(
        _combine_kernel,
        grid=(T // tm,),
        in_specs=[pl.BlockSpec(memory_space=pl.ANY), pl.BlockSpec(memory_space=pl.ANY),
                  pl.BlockSpec((tm, D), lambda i: (i, 0)),
                  pl.BlockSpec((tm, LANE), lambda i: (i, 0)),
                  pl.BlockSpec((1, 6, D), lambda i: (i // per_seq, 0, 0))],
        out_specs=pl.BlockSpec((tm, D), lambda i: (i, 0)),
        out_shape=jax.ShapeDtypeStruct((T, D), F32),
        scratch_shapes=[pltpu.SMEM((SUBLANE, LANE), I32), pltpu.VMEM((DMA_ROWS, D), F32),
                        pltpu.SemaphoreType.DMA(()), pltpu.SemaphoreType.DMA(())],
        compiler_params=_params("arbitrary"),
    )(pos, yb, x1, route, mod3)


def _dispatch_plan(route, T):
    bm = MOE_BLOCK
    e_flat = route[:, :2].astype(I32).reshape(-1)
    onehot = (e_flat[:, None] == jnp.arange(N_EXPERTS, dtype=I32)[None, :]).astype(I32)
    csum = jnp.cumsum(onehot, axis=0)
    rank = jnp.take_along_axis(csum, e_flat[:, None], axis=1)[:, 0] - 1
    counts = csum[-1]
    padded = (counts + bm - 1) // bm * bm
    pend = jnp.cumsum(padded)
    dest = (pend - padded)[e_flat] + rank
    n_blk = (2 * T + bm - 1) // bm + N_EXPERTS
    n_blk = (n_blk * bm + DMA_ROWS - 1) // DMA_ROWS * DMA_ROWS // bm
    row_src = jnp.zeros((n_blk * bm,), I32).at[dest].set(jnp.arange(2 * T, dtype=I32) // 2)
    blk_start = jnp.arange(n_blk, dtype=I32) * bm
    blk_e = jnp.minimum(jnp.sum((pend[None, :] <= blk_start[:, None]).astype(I32), axis=1), N_EXPERTS - 1)
    n_used = (pend[-1:] // bm).astype(I32)
    return dest, row_src, blk_e, n_used


def kernel(x, c, w_ada, b_ada, g_norm_mix, g_norm_ffn, w_in, g_nsa_q, g_nsa_k, pe_cmp, w_cmp1, w_cmp2, g_diff_q, g_diff_k, lam_q1, lam_k1, lam_q2, lam_k2, g_diff_out, w_out, w_router_group, b_router_group, w_router_expert, b_router_expert, w_exp_gate, w_exp_up, w_exp_down):
    B, S, D = x.shape
    T = B * S
    assert w_ada.shape[0] == 1, "single-layer operation"
    assert S % TK_ATT == 0 and S % TM_PROJ == 0 and S % (DMA_ROWS // 2) == 0

    mod3 = _ada(c, w_ada[0], b_ada[0]).reshape(B, 6, D)
    qn, qr, kcv, kv, gates, dq, dkv = _inproj(
        x, mod3, g_norm_mix[0], w_in[0], g_nsa_q[0], g_nsa_k[0], g_diff_q[0], g_diff_k[0])
    kcmp, vcmp = _compress(kcv, pe_cmp[0], w_cmp1[0], w_cmp2[0], g_nsa_k[0, 0])
    o_nsa = _nsa(qn, qr, kcmp, vcmp, kv, gates)
    lam4 = jnp.stack([lam_q1[0], lam_k1[0], lam_q2[0], lam_k2[0]])
    o_diff = _diff(dq, dkv, lam4, g_diff_out[0])
    x1, h2, route = _outproj(x, o_nsa, o_diff, w_out[0], mod3, g_norm_ffn[0],
                             w_router_group[0], b_router_group[0], w_router_expert[0], b_router_expert[0])

    route = route.reshape(T, LANE)
    dest, row_src, blk_e, n_used = _dispatch_plan(route, T)
    buf = _gather_rows(row_src.reshape(-1, SUBLANE, LANE), h2.reshape(T, D))
    yb = _experts(blk_e, n_used, buf, w_exp_gate[0], w_exp_up[0], w_exp_down[0])
    tm = DMA_ROWS // 2
    pos = dest.reshape(T // tm, tm, 2).transpose(0, 2, 1).reshape(T // tm, SUBLANE, LANE)
    out = _combine(pos, yb, x1.reshape(T, D), route, mod3, S)
    return out.reshape(B, S, D)
```

```python
import functools
import math

import numpy as np
import jax
import jax.numpy as jnp
from jax import lax
from jax.experimental import pallas as pl
from jax.experimental.pallas import tpu as pltpu

F32, BF16, I32 = jnp.float32, jnp.bfloat16, jnp.int32
HIGHEST = lax.Precision.HIGHEST

HEAD_DIM = 64
NSA_HEADS = 8
NSA_KV_HEADS = 2
NSA_GROUP = NSA_HEADS // NSA_KV_HEADS
CMP_LEN = 32
CMP_STRIDE = 16
CMP_HIDDEN = 256
SLC_LEN = 64
SLC_TOPK = 6
WIN = 256
DIFF_QK_DIM = 32
DIFF_V_DIM = 64
DIFF_HEADS = 8
ROPE_THETA = 500000.0
N_EXPERT_GROUPS = 4
EXPERTS_PER_GROUP = 8
N_EXPERTS = N_EXPERT_GROUPS * EXPERTS_PER_GROUP
EPS = 1e-6
NEG = -1e30
FORCE_BONUS = 1e4
LAMBDA_INIT = 0.8 - 0.6 * math.exp(-0.3 * 0)

LANE = 128
SUBLANE = 8
VMEM_LIMIT = 48 * 1024 * 1024

TM_PROJ = 512
TQ_ATT = 128
TK_ATT = 512
MOE_BLOCK = 512
DMA_ROWS = SUBLANE * LANE
SENTINEL = -3e38


def _dot(a, b, **kw):
    return jnp.dot(a, b, preferred_element_type=F32, **kw)


def _dot_nt(a, b):
    return lax.dot_general(a, b, (((1,), (1,)), ((), ())), preferred_element_type=F32)


def _params(*sem):
    return pltpu.CompilerParams(dimension_semantics=sem, vmem_limit_bytes=VMEM_LIMIT)


def _ada_kernel(c_ref, w_ref, b_ref, o_ref):
    o_ref[...] = _dot(c_ref[...], w_ref[...], precision=HIGHEST) + b_ref[...]


def _ada(c, w, b):
    B, D = c.shape
    n = w.shape[1]
    return pl.pallas_call(
        _ada_kernel,
        grid=(n // D,),
        in_specs=[pl.BlockSpec((B, D), lambda j: (0, 0)),
                  pl.BlockSpec((D, D), lambda j: (0, j)),
                  pl.BlockSpec((1, D), lambda j: (0, j))],
        out_specs=pl.BlockSpec((B, D), lambda j: (0, j)),
        out_shape=jax.ShapeDtypeStruct((B, n), F32),
        compiler_params=_params("arbitrary"),
    )(c, w, b.reshape(1, n))


_C_Q, _C_KCV, _C_KS, _C_KW, _C_DQ, _C_DK, _C_DV, _C_GATE, _C_END = (
    0, 512, 768, 1024, 1280, 1792, 2304, 2816, 2944)


def _seg_rms(y, bd, inv_n):
    y2 = y * y
    hi = y2.astype(BF16)
    lo = (y2 - hi.astype(F32)).astype(BF16)
    ss = _dot(hi, bd) + _dot(lo, bd)
    return y * lax.rsqrt(ss * inv_n + EPS)


def _rope(y, c, sa, sb, half):
    return y * c + pltpu.roll(y, LANE - half, 1) * sa + pltpu.roll(y, half, 1) * sb


def _inproj_kernel(x_ref, mod_ref, gmix_ref, w_ref, gains_ref, bd_ref, rope_ref,
                   qn_ref, qr_ref, kcv_ref, kv_ref, gate_ref, dq_ref, dkv_ref):
    x = x_ref[0]
    shift, scale = mod_ref[0, 0:1, :], mod_ref[0, 1:2, :]
    ms = jnp.mean(x * x, axis=-1, keepdims=True)
    h = x * lax.rsqrt(ms + EPS) * gmix_ref[...] * (1.0 + scale) + shift
    hb = h.astype(BF16)
    bd64, bd32 = bd_ref[0], bd_ref[1]
    cn, san, sbn = rope_ref[0], rope_ref[1], rope_ref[2]
    cd, sad, sbd = rope_ref[3], rope_ref[4], rope_ref[5]
    half_n, half_d = HEAD_DIM // 8, DIFF_QK_DIM // 8

    def mm(c0):
        y = _dot(hb, w_ref[:, c0:c0 + 2 * LANE])
        return y[:, :LANE], y[:, LANE:]

    for c in range(2):
        for j, y in enumerate(mm(_C_Q + c * 2 * LANE)):
            col = (2 * c + j) * LANE
            yn = _seg_rms(y, bd64, 1.0 / HEAD_DIM) * gains_ref[0:1, :]
            qn_ref[0, :, col:col + LANE] = yn.astype(BF16)
            qr_ref[0, :, col:col + LANE] = _rope(yn, cn, san, sbn, half_n).astype(BF16)

    kc, vc = mm(_C_KCV)
    kcv_ref[0, :, 0:LANE] = kc.astype(BF16)
    kcv_ref[0, :, LANE:2 * LANE] = vc.astype(BF16)
    for i, c0 in enumerate((_C_KS, _C_KW)):
        k, v = mm(c0)
        kn = _seg_rms(k, bd64, 1.0 / HEAD_DIM) * gains_ref[1 + i:2 + i, :]
        kv_ref[0, :, 2 * i * LANE:(2 * i + 1) * LANE] = _rope(kn, cn, san, sbn, half_n).astype(BF16)
        kv_ref[0, :, (2 * i + 1) * LANE:(2 * i + 2) * LANE] = v.astype(BF16)

    for i, (c0, out_ref) in enumerate(((_C_DQ, dq_ref), (_C_DK, dkv_ref))):
        for c in range(2):
            for j, y in enumerate(mm(c0 + c * 2 * LANE)):
                col = (2 * c + j) * LANE
                yn = _seg_rms(y, bd32, 1.0 / DIFF_QK_DIM) * gains_ref[3 + i:4 + i, :]
                out_ref[0, :, col:col + LANE] = _rope(yn, cd, sad, sbd, half_d).astype(BF16)
    for c in range(2):
        for j, y in enumerate(mm(_C_DV + c * 2 * LANE)):
            col = 4 * LANE + (2 * c + j) * LANE
            dkv_ref[0, :, col:col + LANE] = y.astype(BF16)

    g = _dot(hb, w_ref[:, _C_GATE:_C_END])
    gate_ref[0] = 1.0 / (1.0 + jnp.exp(-g))


def _rope_tables(seq, head_dim):
    rot = head_dim // 4
    half = rot // 2
    inv = ROPE_THETA ** (-jnp.arange(0, rot, 2, dtype=F32) / rot)
    ang = jnp.arange(seq, dtype=F32)[:, None] * inv[None, :]
    cos, sin = jnp.cos(ang), jnp.sin(ang)
    pad = jnp.zeros((seq, head_dim - 2 * half), F32)
    c = jnp.concatenate([cos, cos, pad + 1.0], axis=-1)
    sa = jnp.concatenate([-sin, jnp.zeros_like(sin), pad], axis=-1)
    sb = jnp.concatenate([jnp.zeros_like(sin), sin, pad], axis=-1)
    reps = LANE // head_dim
    return jnp.stack([jnp.tile(t, (1, reps)) for t in (c, sa, sb)])


def _block_diag_ones(seg):
    i = np.arange(LANE)
    return jnp.asarray((i[:, None] // seg) == (i[None, :] // seg), dtype=BF16)


def _inproj(x, mod3, g_mix, w_in, g_nsa_q, g_nsa_k, g_diff_q, g_diff_k):
    B, S, D = x.shape
    tm = TM_PROJ
    gate_w = w_in[:, 1280:1304]
    w = jnp.concatenate(
        [w_in[:, :1280], w_in[:, 1304:], jnp.pad(gate_w, ((0, 0), (0, LANE - gate_w.shape[1])))],
        axis=1).astype(BF16)
    scale = HEAD_DIM ** -0.5
    gains = jnp.stack([
        jnp.tile(g_nsa_q, LANE // HEAD_DIM) * scale,
        jnp.tile(g_nsa_k[1], LANE // HEAD_DIM), jnp.tile(g_nsa_k[2], LANE // HEAD_DIM),
        jnp.tile(g_diff_q, LANE // DIFF_QK_DIM), jnp.tile(g_diff_k, LANE // DIFF_QK_DIM),
        jnp.zeros(LANE), jnp.zeros(LANE), jnp.zeros(LANE)]).astype(F32)
    bd = jnp.stack([_block_diag_ones(HEAD_DIM), _block_diag_ones(DIFF_QK_DIM)])
    rope = jnp.concatenate([_rope_tables(S, HEAD_DIM), _rope_tables(S, DIFF_QK_DIM)])

    def tok(width):
        return pl.BlockSpec((1, tm, width), lambda s, b: (b, s, 0))

    def out(width, dt=BF16):
        return jax.ShapeDtypeStruct((B, S, width), dt)

    return pl.pallas_call(
        _inproj_kernel,
        grid=(S // tm, B),
        in_specs=[tok(D),
                  pl.BlockSpec((1, 6, D), lambda s, b: (b, 0, 0)),
                  pl.BlockSpec((1, D), lambda s, b: (0, 0)),
                  pl.BlockSpec((D, _C_END), lambda s, b: (0, 0)),
                  pl.BlockSpec((SUBLANE, LANE), lambda s, b: (0, 0)),
                  pl.BlockSpec((2, LANE, LANE), lambda s, b: (0, 0, 0)),
                  pl.BlockSpec((6, tm, LANE), lambda s, b: (0, s, 0))],
        out_specs=[tok(512), tok(512), tok(256), tok(512), tok(LANE), tok(512), tok(1024)],
        out_shape=[out(512), out(512), out(256), out(512), out(LANE, F32), out(512), out(1024)],
        compiler_params=_params("arbitrary", "arbitrary"),
    )(x, mod3, g_mix.reshape(1, D), w, gains, bd, rope)


def _gelu_tanh(x):
    return 0.5 * x * (1.0 + jnp.tanh(math.sqrt(2.0 / math.pi) * (x + 0.044715 * (x * x * x))))


def _compress_kernel(r_ref, w1_ref, pe_ref, w2_ref, gk_ref, kc_ref, vc_ref):
    half = CMP_STRIDE * HEAD_DIM
    n_rows = r_ref.shape[2]
    for kv, out_ref in enumerate((kc_ref, vc_ref)):
        r = r_ref[0, kv]
        a = _dot(r, w1_ref[kv, 0:half, :])
        b = _dot(r, w1_ref[kv, half:2 * half, :])
        pb = _dot(pe_ref[kv], w1_ref[kv])
        hid = a + pltpu.roll(b, n_rows - 1, 0) + (pb[0:1] + pb[1:2])
        o = _dot(_gelu_tanh(hid).astype(BF16), w2_ref[kv])
        if kv == 0:
            o = o * lax.rsqrt(jnp.mean(o * o, axis=-1, keepdims=True) + EPS) * gk_ref[...]
        for g in range(NSA_KV_HEADS):
            out_ref[0, g] = o[g * LANE:(g + 1) * LANE].astype(BF16)


def _compress(kcv, pe_cmp, w_cmp1, w_cmp2, g_k0):
    B, S, _ = kcv.shape
    n_rows = S // CMP_STRIDE
    width = CMP_STRIDE * HEAD_DIM
    r = kcv.reshape(B, n_rows, CMP_STRIDE, 2, NSA_KV_HEADS, HEAD_DIM)
    r = r.transpose(0, 3, 4, 1, 2, 5).reshape(B, 2, NSA_KV_HEADS * n_rows, width)
    pe = pe_cmp.reshape(2, 1, CMP_LEN * HEAD_DIM)
    pe_hi = pe.astype(BF16)
    pe_lo = (pe - pe_hi.astype(F32)).astype(BF16)
    pe2 = jnp.concatenate([pe_hi, pe_lo, jnp.zeros((2, SUBLANE - 2, CMP_LEN * HEAD_DIM), BF16)], axis=1)
    shape = jax.ShapeDtypeStruct((B, NSA_KV_HEADS, n_rows, HEAD_DIM), BF16)
    spec = pl.BlockSpec((1, NSA_KV_HEADS, n_rows, HEAD_DIM), lambda b: (b, 0, 0, 0))
    return pl.pallas_call(
        _compress_kernel,
        grid=(B,),
        in_specs=[pl.BlockSpec((1, 2, NSA_KV_HEADS * n_rows, width), lambda b: (b, 0, 0, 0)),
                  pl.BlockSpec((2, 2 * width, CMP_HIDDEN), lambda b: (0, 0, 0)),
                  pl.BlockSpec((2, SUBLANE, 2 * width), lambda b: (0, 0, 0)),
                  pl.BlockSpec((2, CMP_HIDDEN, HEAD_DIM), lambda b: (0, 0, 0)),
                  pl.BlockSpec((1, HEAD_DIM), lambda b: (0, 0))],
        out_specs=[spec, spec],
        out_shape=[shape, shape],
        compiler_params=_params("arbitrary"),
    )(r, w_cmp1.astype(BF16), pe2, w_cmp2.astype(BF16), g_k0.reshape(1, HEAD_DIM))


def _softmax_rows(s):
    m = jnp.max(s, axis=-1, keepdims=True)
    e = jnp.exp(s - m)
    return e / jnp.sum(e, axis=-1, keepdims=True)


def _nsa_kernel(qn_ref, qr_ref, kc_ref, vc_ref, kv_ref, gate_ref, ovl_ref, exp_ref,
                o_ref, selbias_ref):
    tq, tk = TQ_ATT, TK_ATT
    qi = pl.program_id(1)
    t_col = qi * tq + lax.broadcasted_iota(I32, (tq, 1), 0)
    lane_i = lax.broadcasted_iota(I32, (1, LANE), 1)
    lane_f = lane_i.astype(F32)
    n_cmp = (kv_ref.shape[1] - CMP_LEN) // CMP_STRIDE + 1
    n_slc = kv_ref.shape[1] // SLC_LEN
    cmp_ok = (lane_i * CMP_STRIDE + (CMP_LEN - 1) <= t_col) & (lane_i < n_cmp)
    cmp_any = (t_col >= CMP_LEN - 1).astype(F32)
    cur = t_col // SLC_LEN
    blk_valid = (lane_i <= cur) & (lane_i < n_slc)
    forced = (lane_i == 0) | (lane_i == cur) | (lane_i == cur - 1)
    n_chunks = (qi * tq + tq + tk - 1) // tk
    w_start = pl.multiple_of(jnp.maximum(qi * tq - WIN, 0), tq)
    w_len = WIN + tq
    w_pos = w_start + lax.broadcasted_iota(I32, (1, w_len), 1)
    w_bias = jnp.where((w_pos <= t_col) & (w_pos > t_col - WIN), 0.0, NEG)
    w_bias4 = jnp.concatenate([w_bias] * NSA_GROUP, axis=0)

    for g in range(NSA_KV_HEADS):
        heads = [g * NSA_GROUP + n for n in range(NSA_GROUP)]
        kc, vc = kc_ref[0, g], vc_ref[0, g]
        psum = jnp.zeros((tq, LANE), F32)
        o_cmp = []
        for h in heads:
            s = _dot_nt(qn_ref[0, :, h * HEAD_DIM:(h + 1) * HEAD_DIM], kc)
            p = _softmax_rows(jnp.where(cmp_ok, s, NEG)) * cmp_any
            psum = psum + p
            o_cmp.append(_dot(p.astype(BF16), vc))
        imp = _dot(psum, ovl_ref[...], precision=HIGHEST)
        score = jnp.where(blk_valid, imp + jnp.where(forced, FORCE_BONUS, 0.0), SENTINEL)
        sel = jnp.zeros((tq, LANE), F32)
        for _ in range(SLC_TOPK):
            m = jnp.max(score, axis=-1, keepdims=True)
            first = jnp.min(jnp.where(score == m, lane_f, float(LANE)), axis=-1, keepdims=True)
            pick = (lane_f == first) & (m > SENTINEL)
            sel = jnp.where(pick, 1.0, sel)
            score = jnp.where(pick, SENTINEL, score)
        selbias_ref[...] = (_dot(sel.astype(BF16), exp_ref[...]) - 1.0) * (-NEG)

        q4 = jnp.concatenate([qr_ref[0, :, h * HEAD_DIM:(h + 1) * HEAD_DIM] for h in heads], axis=0)
        k_col = 0 * LANE + g * HEAD_DIM
        v_col = 1 * LANE + g * HEAD_DIM

        def slc_step(c, carry):
            m, l, acc = carry
            k0 = pl.multiple_of(c * tk, tk)
            k = kv_ref[0, pl.ds(k0, tk), k_col:k_col + HEAD_DIM]
            v = kv_ref[0, pl.ds(k0, tk), v_col:v_col + HEAD_DIM]
            kpos = k0 + lax.broadcasted_iota(I32, (1, tk), 1)
            bias = selbias_ref[:, pl.ds(k0, tk)] + jnp.where(kpos <= t_col, 0.0, NEG)
            s = _dot_nt(q4, k) + jnp.concatenate([bias] * NSA_GROUP, axis=0)
            m_new = jnp.maximum(m, jnp.max(s, axis=-1, keepdims=True))
            alpha = jnp.exp(m - m_new)
            p = jnp.exp(s - m_new)
            l = alpha * l + jnp.sum(p, axis=-1, keepdims=True)
            acc = alpha * acc + _dot(p.astype(BF16), v)
            return m_new, l, acc

        init = (jnp.full((NSA_GROUP * tq, 1), NEG, F32), jnp.zeros((NSA_GROUP * tq, 1), F32),
                jnp.zeros((NSA_GROUP * tq, HEAD_DIM), F32))
        _, l, acc = lax.fori_loop(0, n_chunks, slc_step, init)
        o_slc = acc / l

        kw = kv_ref[0, pl.ds(w_start, w_len), 2 * LANE + g * HEAD_DIM:2 * LANE + (g + 1) * HEAD_DIM]
        vw = kv_ref[0, pl.ds(w_start, w_len), 3 * LANE + g * HEAD_DIM:3 * LANE + (g + 1) * HEAD_DIM]
        pw = _softmax_rows(_dot_nt(q4, kw) + w_bias4)
        o_win = _dot(pw.astype(BF16), vw)

        for n, h in enumerate(heads):
            rows = slice(n * tq, (n + 1) * tq)
            o = (gate_ref[0, :, 3 * h:3 * h + 1] * o_cmp[n]
                 + gate_ref[0, :, 3 * h + 1:3 * h + 2] * o_slc[rows]
                 + gate_ref[0, :, 3 * h + 2:3 * h + 3] * o_win[rows])
            o_ref[0, :, h * HEAD_DIM:(h + 1) * HEAD_DIM] = o.astype(BF16)


def _nsa(qn, qr, kcmp, vcmp, kv, gates):
    B, S, W = qn.shape
    tq = TQ_ATT
    n_cmp = (S - CMP_LEN) // CMP_STRIDE + 1
    n_slc = S // SLC_LEN
    c0 = np.arange(LANE)[:, None] * CMP_STRIDE
    s0 = np.arange(LANE)[None, :] * SLC_LEN
    ovl = np.clip(np.minimum(c0 + CMP_LEN, s0 + SLC_LEN) - np.maximum(c0, s0), 0, None) / CMP_LEN
    ovl = ovl * (np.arange(LANE)[:, None] < n_cmp) * (np.arange(LANE)[None, :] < n_slc)
    expand = (np.arange(S)[None, :] // SLC_LEN) == np.arange(LANE)[:, None]
    cmp_spec = pl.BlockSpec((1,) + kcmp.shape[1:], lambda b, q: (b, 0, 0, 0))
    return pl.pallas_call(
        _nsa_kernel,
        grid=(B, S // tq),
        in_specs=[pl.BlockSpec((1, tq, W), lambda b, q: (b, q, 0)),
                  pl.BlockSpec((1, tq, W), lambda b, q: (b, q, 0)),
                  cmp_spec, cmp_spec,
                  pl.BlockSpec((1, S, kv.shape[2]), lambda b, q: (b, 0, 0)),
                  pl.BlockSpec((1, tq, LANE), lambda b, q: (b, q, 0)),
                  pl.BlockSpec((LANE, LANE), lambda b, q: (0, 0)),
                  pl.BlockSpec((LANE, S), lambda b, q: (0, 0))],
        out_specs=pl.BlockSpec((1, tq, W), lambda b, q: (b, q, 0)),
        out_shape=jax.ShapeDtypeStruct((B, S, W), BF16),
        scratch_shapes=[pltpu.VMEM((tq, S), F32)],
        compiler_params=_params("arbitrary", "arbitrary"),
    )(qn, qr, kcmp, vcmp, kv, gates, jnp.asarray(ovl, F32), jnp.asarray(expand, BF16))


def _diff_kernel(q_ref, kv_ref, lam_ref, go_ref, o_ref):
    tq, tk = TQ_ATT, TK_ATT
    qi = pl.program_id(1)
    t_col = qi * tq + lax.broadcasted_iota(I32, (tq, 1), 0)
    n_chunks = (qi * tq + tq + tk - 1) // tk
    scale = DIFF_QK_DIM ** -0.5
    lq = lam_ref[...]
    lam = (jnp.exp(jnp.sum(lq[0:1] * lq[1:2], axis=-1, keepdims=True))
           - jnp.exp(jnp.sum(lq[2:3] * lq[3:4], axis=-1, keepdims=True)) + LAMBDA_INIT)
    v_base = DIFF_HEADS * 2 * DIFF_QK_DIM

    for h in range(DIFF_HEADS):
        cols = [h * 2 * DIFF_QK_DIM + m * DIFF_QK_DIM for m in range(2)]
        q = [q_ref[0, :, c:c + DIFF_QK_DIM] for c in cols]

        def step(c, carry):
            k0 = pl.multiple_of(c * tk, tk)
            kpos = k0 + lax.broadcasted_iota(I32, (1, tk), 1)
            bias = jnp.where(kpos <= t_col, 0.0, NEG)
            v = kv_ref[0, pl.ds(k0, tk), v_base + h * DIFF_V_DIM:v_base + (h + 1) * DIFF_V_DIM]
            new = []
            for mp in range(2):
                m, l, acc = carry[mp]
                k = kv_ref[0, pl.ds(k0, tk), cols[mp]:cols[mp] + DIFF_QK_DIM]
                s = _dot_nt(q[mp], k) * scale + bias
                m_new = jnp.maximum(m, jnp.max(s, axis=-1, keepdims=True))
                alpha = jnp.exp(m - m_new)
                p = jnp.exp(s - m_new)
                l = alpha * l + jnp.sum(p, axis=-1, keepdims=True)
                acc = alpha * acc + _dot(p.astype(BF16), v)
                new.append((m_new, l, acc))
            return tuple(new)

        one = (jnp.full((tq, 1), NEG, F32), jnp.zeros((tq, 1), F32), jnp.zeros((tq, DIFF_V_DIM), F32))
        (_, l0, a0), (_, l1, a1) = lax.fori_loop(0, n_chunks, step, (one, one))
        o = a0 / l0 - lam * (a1 / l1)
        o = o * lax.rsqrt(jnp.mean(o * o, axis=-1, keepdims=True) + EPS) * go_ref[...] * (1.0 - LAMBDA_INIT)
        o_ref[0, :, h * DIFF_V_DIM:(h + 1) * DIFF_V_DIM] = o.astype(BF16)


def _diff(dq, dkv, lam4, g_out):
    B, S, W = dq.shape
    tq = TQ_ATT
    return pl.pallas_call(
        _diff_kernel,
        grid=(B, S // tq),
        in_specs=[pl.BlockSpec((1, tq, W), lambda b, q: (b, q, 0)),
                  pl.BlockSpec((1, S, dkv.shape[2]), lambda b, q: (b, 0, 0)),
                  pl.BlockSpec(lam4.shape, lambda b, q: (0, 0)),
                  pl.BlockSpec((1, DIFF_V_DIM), lambda b, q: (0, 0))],
        out_specs=pl.BlockSpec((1, tq, W), lambda b, q: (b, q, 0)),
        out_shape=jax.ShapeDtypeStruct((B, S, W), BF16),
        compiler_params=_params("arbitrary", "arbitrary"),
    )(dq, dkv, lam4, g_out.reshape(1, DIFF_V_DIM))


def _outproj_kernel(x_ref, on_ref, od_ref, wo_ref, mod_ref, gffn_ref, wr_ref, br_ref,
                    x1_ref, h_ref, route_ref):
    half = on_ref.shape[2]
    gate1 = mod_ref[0, 2:3, :]
    shift2, scale2 = mod_ref[0, 3:4, :], mod_ref[0, 4:5, :]
    attn = _dot(on_ref[0], wo_ref[0:half, :]) + _dot(od_ref[0], wo_ref[half:2 * half, :])
    x1 = x_ref[0] + gate1 * attn
    x1_ref[0] = x1
    ms = jnp.mean(x1 * x1, axis=-1, keepdims=True)
    h = x1 * lax.rsqrt(ms + EPS) * gffn_ref[...] * (1.0 + scale2) + shift2
    h_ref[0] = h

    r = _dot(h, wr_ref[...], precision=HIGHEST) + br_ref[...]
    lg, le = r[:, :LANE], r[:, LANE:]
    lane = lax.broadcasted_iota(I32, (1, LANE), 1).astype(F32)
    lg = jnp.where(lane < N_EXPERT_GROUPS, lg, SENTINEL)
    mg = jnp.max(lg, axis=-1, keepdims=True)
    p_grp = 1.0 / jnp.sum(jnp.exp(lg - mg), axis=-1, keepdims=True)
    grp = jnp.min(jnp.where(lg == mg, lane, float(LANE)), axis=-1, keepdims=True)
    lo = grp * EXPERTS_PER_GROUP
    le = jnp.where((lane >= lo) & (lane < lo + EXPERTS_PER_GROUP), le, SENTINEL)
    v1 = jnp.max(le, axis=-1, keepdims=True)
    i1 = jnp.min(jnp.where(le == v1, lane, float(LANE)), axis=-1, keepdims=True)
    le = jnp.where(lane == i1, SENTINEL, le)
    v2 = jnp.max(le, axis=-1, keepdims=True)
    i2 = jnp.min(jnp.where(le == v2, lane, float(LANE)), axis=-1, keepdims=True)
    e2 = jnp.exp(v2 - v1)
    w1 = p_grp / (1.0 + e2)
    w2 = p_grp * e2 / (1.0 + e2)
    route_ref[0] = jnp.where(lane == 0, i1, jnp.where(lane == 1, i2,
                             jnp.where(lane == 2, w1, jnp.where(lane == 3, w2, 0.0))))


def _outproj(x, o_nsa, o_diff, w_out, mod3, g_ffn, w_rg, b_rg, w_re, b_re):
    B, S, D = x.shape
    tm = TM_PROJ
    half = o_nsa.shape[2]
    wr = jnp.zeros((D, 2 * LANE), F32).at[:, :N_EXPERT_GROUPS].set(w_rg)
    wr = wr.at[:, LANE:LANE + N_EXPERTS].set(w_re)
    br = jnp.zeros((1, 2 * LANE), F32).at[0, :N_EXPERT_GROUPS].set(b_rg)
    br = br.at[0, LANE:LANE + N_EXPERTS].set(b_re)

    def tok(width):
        return pl.BlockSpec((1, tm, width), lambda b, s: (b, s, 0))

    return pl.pallas_call(
        _outproj_kernel,
        grid=(B, S // tm),
        in_specs=[tok(D), tok(half), tok(half),
                  pl.BlockSpec((D, D), lambda b, s: (0, 0)),
                  pl.BlockSpec((1, 6, D), lambda b, s: (b, 0, 0)),
                  pl.BlockSpec((1, D), lambda b, s: (0, 0)),
                  pl.BlockSpec((D, 2 * LANE), lambda b, s: (0, 0)),
                  pl.BlockSpec((1, 2 * LANE), lambda b, s: (0, 0))],
        out_specs=[tok(D), tok(D), tok(LANE)],
        out_shape=[jax.ShapeDtypeStruct((B, S, D), F32), jax.ShapeDtypeStruct((B, S, D), F32),
                   jax.ShapeDtypeStruct((B, S, LANE), F32)],
        compiler_params=_params("arbitrary", "arbitrary"),
    )(x, o_nsa, o_diff, w_out.astype(BF16), mod3, g_ffn.reshape(1, D), wr, br)


def _row_copy(src_hbm, dst, row, slot, sem):
    return pltpu.make_async_copy(src_hbm.at[pl.ds(row, 1)], dst.at[pl.ds(slot, 1)], sem)


def _prefetch_rows(i, n_steps, n_rows, idx_hbm, idx_smem, idx_sem, src_hbm, rows, row_sem):
    slot = i % 2

    def idx_copy(step, s):
        return pltpu.make_async_copy(idx_hbm.at[step], idx_smem.at[s], idx_sem.at[s])

    def issue_rows(s):
        def body(r, carry):
            _row_copy(src_hbm, rows.at[s], idx_smem[s, r // LANE, r % LANE], r, row_sem.at[s]).start()
            return carry
        lax.fori_loop(0, n_rows, body, 0, unroll=8)

    @pl.when(i == 0)
    def _():
        idx_copy(0, 0).start()
        idx_copy(0, 0).wait()
        issue_rows(0)

        @pl.when(n_steps > 1)
        def _():
            idx_copy(1, 1).start()

    @pl.when(i + 1 < n_steps)
    def _():
        idx_copy(i + 1, 1 - slot).wait()
        issue_rows(1 - slot)

        @pl.when(i + 2 < n_steps)
        def _():
            idx_copy(i + 2, slot).start()


def _wait_rows(n_rows, src_hbm, dst, sem):
    pltpu.make_async_copy(src_hbm.at[pl.ds(0, n_rows)], dst, sem).wait()


def _expert_kernel(blk_e_ref, n_used_ref, idx_hbm, h_hbm, wg_ref, wu_ref, wd_ref, o_ref,
                   idx_smem, xbuf, idx_sem, row_sem):
    i = pl.program_id(0)
    n_used = n_used_ref[0]
    slot = i % 2
    _prefetch_rows(i, n_used, MOE_BLOCK, idx_hbm, idx_smem, idx_sem, h_hbm, xbuf, row_sem)

    @pl.when(i < n_used)
    def _():
        _wait_rows(MOE_BLOCK, h_hbm, xbuf.at[slot], row_sem.at[slot])
        xb = xbuf[slot].astype(BF16)
        a = _dot(xb, wg_ref[0])
        u = _dot(xb, wu_ref[0])
        act = a * (1.0 / (1.0 + jnp.exp(-a))) * u
        o_ref[...] = _dot(act.astype(BF16), wd_ref[0])

    @pl.when(i >= n_used)
    def _():
        o_ref[...] = jnp.zeros_like(o_ref)


def _experts(blk_e, n_used, row_src, h, w_g, w_u, w_d):
    bm = MOE_BLOCK
    n_blk = row_src.shape[0]
    D = h.shape[1]
    H = w_g.shape[2]
    grid_spec = pltpu.PrefetchScalarGridSpec(
        num_scalar_prefetch=2,
        grid=(n_blk,),
        in_specs=[pl.BlockSpec(memory_space=pl.ANY), pl.BlockSpec(memory_space=pl.ANY),
                  pl.BlockSpec((1, D, H), lambda i, be, nu: (be[i], 0, 0)),
                  pl.BlockSpec((1, D, H), lambda i, be, nu: (be[i], 0, 0)),
                  pl.BlockSpec((1, H, D), lambda i, be, nu: (be[i], 0, 0))],
        out_specs=pl.BlockSpec((bm, D), lambda i, be, nu: (i, 0)),
        scratch_shapes=[pltpu.SMEM((2, SUBLANE, LANE), I32), pltpu.VMEM((2, bm, D), h.dtype),
                        pltpu.SemaphoreType.DMA((2,)), pltpu.SemaphoreType.DMA((2,))])
    return pl.pallas_call(
        _expert_kernel,
        grid_spec=grid_spec,
        out_shape=jax.ShapeDtypeStruct((n_blk * bm, D), F32),
        compiler_params=_params("arbitrary"),
    )(blk_e, n_used, row_src, h, w_g.astype(BF16), w_u.astype(BF16), w_d.astype(BF16))


def _combine_kernel(pos_hbm, yb_hbm, x1_ref, route_ref, mod_ref, o_ref, idx_smem, rows_ref, idx_sem, row_sem):
    i = pl.program_id(0)
    slot = i % 2
    _prefetch_rows(i, pl.num_programs(0), DMA_ROWS, pos_hbm, idx_smem, idx_sem, yb_hbm, rows_ref, row_sem)
    _wait_rows(DMA_ROWS, yb_hbm, rows_ref.at[slot], row_sem.at[slot])
    tm = DMA_ROWS // 2
    gate2 = mod_ref[0, 5:6, :]
    y = rows_ref[slot, 0:tm, :] * route_ref[:, 2:3] + rows_ref[slot, tm:2 * tm, :] * route_ref[:, 3:4]
    o_ref[...] = x1_ref[...] + gate2 * y


def _combine(pos, yb, x1, route, mod3, seq):
    T, D = x1.shape
    tm = DMA_ROWS // 2
    per_seq = seq // tm
    return pl.pallas_call(
        _combine_kernel,
        grid=(T // tm,),
        in_specs=[pl.BlockSpec(memory_space=pl.ANY), pl.BlockSpec(memory_space=pl.ANY),
                  pl.BlockSpec((tm, D), lambda i: (i, 0)),
                  pl.BlockSpec((tm, LANE), lambda i: (i, 0)),
                  pl.BlockSpec((1, 6, D), lambda i: (i // per_seq, 0, 0))],
        out_specs=pl.BlockSpec((tm, D), lambda i: (i, 0)),
        out_shape=jax.ShapeDtypeStruct((T, D), F32),
        scratch_shapes=[pltpu.SMEM((2, SUBLANE, LANE), I32), pltpu.VMEM((2, DMA_ROWS, D), F32),
                        pltpu.SemaphoreType.DMA((2,)), pltpu.SemaphoreType.DMA((2,))],
        compiler_params=_params("arbitrary"),
    )(pos, yb, x1, route, mod3)


def _dispatch_plan(route, T):
    bm = MOE_BLOCK
    e_flat = route[:, :2].astype(I32).reshape(-1)
    onehot = (e_flat[:, None] == jnp.arange(N_EXPERTS, dtype=I32)[None, :]).astype(I32)
    csum = jnp.cumsum(onehot, axis=0)
    rank = jnp.take_along_axis(csum, e_flat[:, None], axis=1)[:, 0] - 1
    counts = csum[-1]
    padded = (counts + bm - 1) // bm * bm
    pend = jnp.cumsum(padded)
    dest = (pend - padded)[e_flat] + rank
    n_blk = (2 * T + bm - 1) // bm + N_EXPERTS
    n_blk = (n_blk * bm + DMA_ROWS - 1) // DMA_ROWS * DMA_ROWS // bm
    row_src = jnp.zeros((n_blk * bm,), I32).at[dest].set(jnp.arange(2 * T, dtype=I32) // 2)
    blk_start = jnp.arange(n_blk, dtype=I32) * bm
    blk_e = jnp.minimum(jnp.sum((pend[None, :] <= blk_start[:, None]).astype(I32), axis=1), N_EXPERTS - 1)
    n_used = (pend[-1:] // bm).astype(I32)
    return dest, row_src, blk_e, n_used


def kernel(x, c, w_ada, b_ada, g_norm_mix, g_norm_ffn, w_in, g_nsa_q, g_nsa_k, pe_cmp, w_cmp1, w_cmp2, g_diff_q, g_diff_k, lam_q1, lam_k1, lam_q2, lam_k2, g_diff_out, w_out, w_router_group, b_router_group, w_router_expert, b_router_expert, w_exp_gate, w_exp_up, w_exp_down):
    B, S, D = x.shape
    T = B * S
    assert w_ada.shape[0] == 1, "single-layer operation"
    assert S % TK_ATT == 0 and S % TM_PROJ == 0 and S % (DMA_ROWS // 2) == 0

    mod3 = _ada(c, w_ada[0], b_ada[0]).reshape(B, 6, D)
    qn, qr, kcv, kv, gates, dq, dkv = _inproj(
        x, mod3, g_norm_mix[0], w_in[0], g_nsa_q[0], g_nsa_k[0], g_diff_q[0], g_diff_k[0])
    kcmp, vcmp = _compress(kcv, pe_cmp[0], w_cmp1[0], w_cmp2[0], g_nsa_k[0, 0])
    o_nsa = _nsa(qn, qr, kcmp, vcmp, kv, gates)
    lam4 = jnp.stack([lam_q1[0], lam_k1[0], lam_q2[0], lam_k2[0]])
    o_diff = _diff(dq, dkv, lam4, g_diff_out[0])
    x1, h2, route = _outproj(x, o_nsa, o_diff, w_out[0], mod3, g_norm_ffn[0],
                             w_router_group[0], b_router_group[0], w_router_expert[0], b_router_expert[0])

    route = route.reshape(T, LANE)
    dest, row_src, blk_e, n_used = _dispatch_plan(route, T)
    idx_rows = MOE_BLOCK // LANE
    row_src = jnp.pad(row_src.reshape(-1, idx_rows, LANE), ((0, 0), (0, SUBLANE - idx_rows), (0, 0)))
    yb = _experts(blk_e, n_used, row_src, h2.reshape(T, D), w_exp_gate[0], w_exp_up[0], w_exp_down[0])
    tm = DMA_ROWS // 2
    pos = dest.reshape(T // tm, tm, 2).transpose(0, 2, 1).reshape(T // tm, SUBLANE, LANE)
    out = _combine(pos, yb, x1.reshape(T, D), route, mod3, S)
    return out.reshape(B, S, D)
```

```python
import functools
import math

import numpy as np
import jax
import jax.numpy as jnp
from jax import lax
from jax.experimental import pallas as pl
from jax.experimental.pallas import tpu as pltpu

F32, BF16, I32 = jnp.float32, jnp.bfloat16, jnp.int32
HIGHEST = lax.Precision.HIGHEST

HEAD_DIM = 64
NSA_HEADS = 8
NSA_KV_HEADS = 2
NSA_GROUP = NSA_HEADS // NSA_KV_HEADS
CMP_LEN = 32
CMP_STRIDE = 16
CMP_HIDDEN = 256
SLC_LEN = 64
SLC_TOPK = 6
WIN = 256
DIFF_QK_DIM = 32
DIFF_V_DIM = 64
DIFF_HEADS = 8
ROPE_THETA = 500000.0
N_EXPERT_GROUPS = 4
EXPERTS_PER_GROUP = 8
N_EXPERTS = N_EXPERT_GROUPS * EXPERTS_PER_GROUP
EPS = 1e-6
NEG = -1e30
FORCE_BONUS = 1e4
LAMBDA_INIT = 0.8 - 0.6 * math.exp(-0.3 * 0)

LANE = 128
SUBLANE = 8
VMEM_LIMIT = 48 * 1024 * 1024

TM_PROJ = 512
TQ_ATT = 128
TK_ATT = 512
TQ_DIFF = 256
MOE_BLOCK = 512
DMA_ROWS = SUBLANE * LANE
SENTINEL = -3e38


def _dot(a, b, **kw):
    return jnp.dot(a, b, preferred_element_type=F32, **kw)


def _dot_nt(a, b):
    return lax.dot_general(a, b, (((1,), (1,)), ((), ())), preferred_element_type=F32)


def _params(*sem):
    return pltpu.CompilerParams(dimension_semantics=sem, vmem_limit_bytes=VMEM_LIMIT)


def _ada_kernel(c_ref, w_ref, b_ref, o_ref):
    o_ref[...] = _dot(c_ref[...], w_ref[...], precision=HIGHEST) + b_ref[...]


def _ada(c, w, b):
    B, D = c.shape
    n = w.shape[1]
    return pl.pallas_call(
        _ada_kernel,
        grid=(n // D,),
        in_specs=[pl.BlockSpec((B, D), lambda j: (0, 0)),
                  pl.BlockSpec((D, D), lambda j: (0, j)),
                  pl.BlockSpec((1, D), lambda j: (0, j))],
        out_specs=pl.BlockSpec((B, D), lambda j: (0, j)),
        out_shape=jax.ShapeDtypeStruct((B, n), F32),
        compiler_params=_params("arbitrary"),
    )(c, w, b.reshape(1, n))


_C_Q, _C_KCV, _C_KS, _C_KW, _C_DQ, _C_DK, _C_DV, _C_GATE, _C_END = (
    0, 512, 768, 1024, 1280, 1792, 2304, 2816, 2944)


def _seg_rms(y, bd, inv_n):
    y2 = y * y
    hi = y2.astype(BF16)
    lo = (y2 - hi.astype(F32)).astype(BF16)
    ss = _dot(hi, bd) + _dot(lo, bd)
    return y * lax.rsqrt(ss * inv_n + EPS)


def _rope(y, c, sa, sb, half):
    return y * c + pltpu.roll(y, LANE - half, 1) * sa + pltpu.roll(y, half, 1) * sb


def _inproj_kernel(x_ref, mod_ref, gmix_ref, w_ref, gains_ref, bd_ref, rope_ref,
                   qn_ref, qr_ref, kcv_ref, kv_ref, gate_ref, dq_ref, dkv_ref):
    x = x_ref[0]
    shift, scale = mod_ref[0, 0:1, :], mod_ref[0, 1:2, :]
    ms = jnp.mean(x * x, axis=-1, keepdims=True)
    h = x * lax.rsqrt(ms + EPS) * gmix_ref[...] * (1.0 + scale) + shift
    hb = h.astype(BF16)
    bd64, bd32 = bd_ref[0], bd_ref[1]
    cn, san, sbn = rope_ref[0], rope_ref[1], rope_ref[2]
    cd, sad, sbd = rope_ref[3], rope_ref[4], rope_ref[5]
    half_n, half_d = HEAD_DIM // 8, DIFF_QK_DIM // 8

    def mm(c0):
        y = _dot(hb, w_ref[:, c0:c0 + 2 * LANE])
        return y[:, :LANE], y[:, LANE:]

    for c in range(2):
        for j, y in enumerate(mm(_C_Q + c * 2 * LANE)):
            col = (2 * c + j) * LANE
            yn = _seg_rms(y, bd64, 1.0 / HEAD_DIM) * gains_ref[0:1, :]
            qn_ref[0, :, col:col + LANE] = yn.astype(BF16)
            qr_ref[0, :, col:col + LANE] = _rope(yn, cn, san, sbn, half_n).astype(BF16)

    kc, vc = mm(_C_KCV)
    kcv_ref[0, :, 0:LANE] = kc.astype(BF16)
    kcv_ref[0, :, LANE:2 * LANE] = vc.astype(BF16)
    for i, c0 in enumerate((_C_KS, _C_KW)):
        k, v = mm(c0)
        kn = _seg_rms(k, bd64, 1.0 / HEAD_DIM) * gains_ref[1 + i:2 + i, :]
        kv_ref[0, :, 2 * i * LANE:(2 * i + 1) * LANE] = _rope(kn, cn, san, sbn, half_n).astype(BF16)
        kv_ref[0, :, (2 * i + 1) * LANE:(2 * i + 2) * LANE] = v.astype(BF16)

    for i, (c0, out_ref) in enumerate(((_C_DQ, dq_ref), (_C_DK, dkv_ref))):
        for c in range(2):
            for j, y in enumerate(mm(c0 + c * 2 * LANE)):
                col = (2 * c + j) * LANE
                yn = _seg_rms(y, bd32, 1.0 / DIFF_QK_DIM) * gains_ref[3 + i:4 + i, :]
                out_ref[0, :, col:col + LANE] = _rope(yn, cd, sad, sbd, half_d).astype(BF16)
    for c in range(2):
        for j, y in enumerate(mm(_C_DV + c * 2 * LANE)):
            col = 4 * LANE + (2 * c + j) * LANE
            dkv_ref[0, :, col:col + LANE] = y.astype(BF16)

    g = _dot(hb, w_ref[:, _C_GATE:_C_END])
    gate_ref[0] = 1.0 / (1.0 + jnp.exp(-g))


def _rope_tables(seq, head_dim):
    rot = head_dim // 4
    half = rot // 2
    inv = ROPE_THETA ** (-jnp.arange(0, rot, 2, dtype=F32) / rot)
    ang = jnp.arange(seq, dtype=F32)[:, None] * inv[None, :]
    cos, sin = jnp.cos(ang), jnp.sin(ang)
    pad = jnp.zeros((seq, head_dim - 2 * half), F32)
    c = jnp.concatenate([cos, cos, pad + 1.0], axis=-1)
    sa = jnp.concatenate([-sin, jnp.zeros_like(sin), pad], axis=-1)
    sb = jnp.concatenate([jnp.zeros_like(sin), sin, pad], axis=-1)
    reps = LANE // head_dim
    return jnp.stack([jnp.tile(t, (1, reps)) for t in (c, sa, sb)])


def _block_diag_ones(seg):
    i = np.arange(LANE)
    return jnp.asarray((i[:, None] // seg) == (i[None, :] // seg), dtype=BF16)


def _inproj(x, mod3, g_mix, w_in, g_nsa_q, g_nsa_k, g_diff_q, g_diff_k):
    B, S, D = x.shape
    tm = TM_PROJ
    gate_w = w_in[:, 1280:1304]
    w = jnp.concatenate(
        [w_in[:, :1280], w_in[:, 1304:], jnp.pad(gate_w, ((0, 0), (0, LANE - gate_w.shape[1])))],
        axis=1).astype(BF16)
    scale = HEAD_DIM ** -0.5
    gains = jnp.stack([
        jnp.tile(g_nsa_q, LANE // HEAD_DIM) * scale,
        jnp.tile(g_nsa_k[1], LANE // HEAD_DIM), jnp.tile(g_nsa_k[2], LANE // HEAD_DIM),
        jnp.tile(g_diff_q, LANE // DIFF_QK_DIM), jnp.tile(g_diff_k, LANE // DIFF_QK_DIM),
        jnp.zeros(LANE), jnp.zeros(LANE), jnp.zeros(LANE)]).astype(F32)
    bd = jnp.stack([_block_diag_ones(HEAD_DIM), _block_diag_ones(DIFF_QK_DIM)])
    rope = jnp.concatenate([_rope_tables(S, HEAD_DIM), _rope_tables(S, DIFF_QK_DIM)])

    def tok(width):
        return pl.BlockSpec((1, tm, width), lambda s, b: (b, s, 0))

    def out(width, dt=BF16):
        return jax.ShapeDtypeStruct((B, S, width), dt)

    return pl.pallas_call(
        _inproj_kernel,
        grid=(S // tm, B),
        in_specs=[tok(D),
                  pl.BlockSpec((1, 6, D), lambda s, b: (b, 0, 0)),
                  pl.BlockSpec((1, D), lambda s, b: (0, 0)),
                  pl.BlockSpec((D, _C_END), lambda s, b: (0, 0)),
                  pl.BlockSpec((SUBLANE, LANE), lambda s, b: (0, 0)),
                  pl.BlockSpec((2, LANE, LANE), lambda s, b: (0, 0, 0)),
                  pl.BlockSpec((6, tm, LANE), lambda s, b: (0, s, 0))],
        out_specs=[tok(512), tok(512), tok(256), tok(512), tok(LANE), tok(512), tok(1024)],
        out_shape=[out(512), out(512), out(256), out(512), out(LANE, F32), out(512), out(1024)],
        compiler_params=_params("arbitrary", "arbitrary"),
    )(x, mod3, g_mix.reshape(1, D), w, gains, bd, rope)


def _gelu_tanh(x):
    return 0.5 * x * (1.0 + jnp.tanh(math.sqrt(2.0 / math.pi) * (x + 0.044715 * (x * x * x))))


def _compress_kernel(r_ref, w1_ref, pe_ref, w2_ref, gk_ref, kc_ref, vc_ref):
    half = CMP_STRIDE * HEAD_DIM
    n_rows = r_ref.shape[2]
    for kv, out_ref in enumerate((kc_ref, vc_ref)):
        r = r_ref[0, kv]
        a = _dot(r, w1_ref[kv, 0:half, :])
        b = _dot(r, w1_ref[kv, half:2 * half, :])
        pb = _dot(pe_ref[kv], w1_ref[kv])
        hid = a + pltpu.roll(b, n_rows - 1, 0) + (pb[0:1] + pb[1:2])
        o = _dot(_gelu_tanh(hid).astype(BF16), w2_ref[kv])
        if kv == 0:
            o = o * lax.rsqrt(jnp.mean(o * o, axis=-1, keepdims=True) + EPS) * gk_ref[...]
        for g in range(NSA_KV_HEADS):
            out_ref[0, g] = o[g * LANE:(g + 1) * LANE].astype(BF16)


def _compress(kcv, pe_cmp, w_cmp1, w_cmp2, g_k0):
    B, S, _ = kcv.shape
    n_rows = S // CMP_STRIDE
    width = CMP_STRIDE * HEAD_DIM
    r = kcv.reshape(B, n_rows, CMP_STRIDE, 2, NSA_KV_HEADS, HEAD_DIM)
    r = r.transpose(0, 3, 4, 1, 2, 5).reshape(B, 2, NSA_KV_HEADS * n_rows, width)
    pe = pe_cmp.reshape(2, 1, CMP_LEN * HEAD_DIM)
    pe_hi = pe.astype(BF16)
    pe_lo = (pe - pe_hi.astype(F32)).astype(BF16)
    pe2 = jnp.concatenate([pe_hi, pe_lo, jnp.zeros((2, SUBLANE - 2, CMP_LEN * HEAD_DIM), BF16)], axis=1)
    shape = jax.ShapeDtypeStruct((B, NSA_KV_HEADS, n_rows, HEAD_DIM), BF16)
    spec = pl.BlockSpec((1, NSA_KV_HEADS, n_rows, HEAD_DIM), lambda b: (b, 0, 0, 0))
    return pl.pallas_call(
        _compress_kernel,
        grid=(B,),
        in_specs=[pl.BlockSpec((1, 2, NSA_KV_HEADS * n_rows, width), lambda b: (b, 0, 0, 0)),
                  pl.BlockSpec((2, 2 * width, CMP_HIDDEN), lambda b: (0, 0, 0)),
                  pl.BlockSpec((2, SUBLANE, 2 * width), lambda b: (0, 0, 0)),
                  pl.BlockSpec((2, CMP_HIDDEN, HEAD_DIM), lambda b: (0, 0, 0)),
                  pl.BlockSpec((1, HEAD_DIM), lambda b: (0, 0))],
        out_specs=[spec, spec],
        out_shape=[shape, shape],
        compiler_params=_params("arbitrary"),
    )(r, w_cmp1.astype(BF16), pe2, w_cmp2.astype(BF16), g_k0.reshape(1, HEAD_DIM))


def _softmax_rows(s):
    m = jnp.max(s, axis=-1, keepdims=True)
    e = jnp.exp(s - m)
    return e / jnp.sum(e, axis=-1, keepdims=True)


def _nsa_kernel(qn_ref, qr_ref, kc_ref, vc_ref, kv_ref, gate_ref, ovl_ref, exp_ref,
                o_ref, selbias_ref):
    tq, tk = TQ_ATT, TK_ATT
    qi = pl.program_id(1)
    t_col = qi * tq + lax.broadcasted_iota(I32, (tq, 1), 0)
    lane_i = lax.broadcasted_iota(I32, (1, LANE), 1)
    lane_f = lane_i.astype(F32)
    n_cmp = (kv_ref.shape[1] - CMP_LEN) // CMP_STRIDE + 1
    n_slc = kv_ref.shape[1] // SLC_LEN
    cmp_ok = (lane_i * CMP_STRIDE + (CMP_LEN - 1) <= t_col) & (lane_i < n_cmp)
    cmp_any = (t_col >= CMP_LEN - 1).astype(F32)
    cur = t_col // SLC_LEN
    blk_valid = (lane_i <= cur) & (lane_i < n_slc)
    forced = (lane_i == 0) | (lane_i == cur) | (lane_i == cur - 1)
    n_chunks = (qi * tq + tq + tk - 1) // tk
    w_start = pl.multiple_of(jnp.maximum(qi * tq - WIN, 0), tq)
    w_len = WIN + tq
    w_pos = w_start + lax.broadcasted_iota(I32, (1, w_len), 1)
    w_bias = jnp.where((w_pos <= t_col) & (w_pos > t_col - WIN), 0.0, NEG)
    w_bias4 = jnp.concatenate([w_bias] * NSA_GROUP, axis=0)

    for g in range(NSA_KV_HEADS):
        heads = [g * NSA_GROUP + n for n in range(NSA_GROUP)]
        kc, vc = kc_ref[0, g], vc_ref[0, g]
        psum = jnp.zeros((tq, LANE), F32)
        o_cmp = []
        for h in heads:
            s = _dot_nt(qn_ref[0, :, h * HEAD_DIM:(h + 1) * HEAD_DIM], kc)
            p = _softmax_rows(jnp.where(cmp_ok, s, NEG)) * cmp_any
            psum = psum + p
            o_cmp.append(_dot(p.astype(BF16), vc))
        imp = _dot(psum, ovl_ref[...], precision=HIGHEST)
        score = jnp.where(blk_valid, imp + jnp.where(forced, FORCE_BONUS, 0.0), SENTINEL)
        sel = jnp.zeros((tq, LANE), F32)
        for _ in range(SLC_TOPK):
            m = jnp.max(score, axis=-1, keepdims=True)
            first = jnp.min(jnp.where(score == m, lane_f, float(LANE)), axis=-1, keepdims=True)
            pick = (lane_f == first) & (m > SENTINEL)
            sel = jnp.where(pick, 1.0, sel)
            score = jnp.where(pick, SENTINEL, score)
        selbias_ref[...] = (_dot(sel.astype(BF16), exp_ref[...]) - 1.0) * (-NEG)

        q4 = jnp.concatenate([qr_ref[0, :, h * HEAD_DIM:(h + 1) * HEAD_DIM] for h in heads], axis=0)
        k_col = 0 * LANE + g * HEAD_DIM
        v_col = 1 * LANE + g * HEAD_DIM

        def slc_step(c, carry):
            m, l, acc = carry
            k0 = pl.multiple_of(c * tk, tk)
            k = kv_ref[0, pl.ds(k0, tk), k_col:k_col + HEAD_DIM]
            v = kv_ref[0, pl.ds(k0, tk), v_col:v_col + HEAD_DIM]
            kpos = k0 + lax.broadcasted_iota(I32, (1, tk), 1)
            bias = selbias_ref[:, pl.ds(k0, tk)] + jnp.where(kpos <= t_col, 0.0, NEG)
            s = _dot_nt(q4, k) + jnp.concatenate([bias] * NSA_GROUP, axis=0)
            m_new = jnp.maximum(m, jnp.max(s, axis=-1, keepdims=True))
            alpha = jnp.exp(m - m_new)
            p = jnp.exp(s - m_new)
            l = alpha * l + jnp.sum(p, axis=-1, keepdims=True)
            acc = alpha * acc + _dot(p.astype(BF16), v)
            return m_new, l, acc

        init = (jnp.full((NSA_GROUP * tq, 1), NEG, F32), jnp.zeros((NSA_GROUP * tq, 1), F32),
                jnp.zeros((NSA_GROUP * tq, HEAD_DIM), F32))
        _, l, acc = lax.fori_loop(0, n_chunks, slc_step, init)
        o_slc = acc / l

        kw = kv_ref[0, pl.ds(w_start, w_len), 2 * LANE + g * HEAD_DIM:2 * LANE + (g + 1) * HEAD_DIM]
        vw = kv_ref[0, pl.ds(w_start, w_len), 3 * LANE + g * HEAD_DIM:3 * LANE + (g + 1) * HEAD_DIM]
        pw = _softmax_rows(_dot_nt(q4, kw) + w_bias4)
        o_win = _dot(pw.astype(BF16), vw)

        for n, h in enumerate(heads):
            rows = slice(n * tq, (n + 1) * tq)
            o = (gate_ref[0, :, 3 * h:3 * h + 1] * o_cmp[n]
                 + gate_ref[0, :, 3 * h + 1:3 * h + 2] * o_slc[rows]
                 + gate_ref[0, :, 3 * h + 2:3 * h + 3] * o_win[rows])
            o_ref[0, :, h * HEAD_DIM:(h + 1) * HEAD_DIM] = o.astype(BF16)


def _nsa(qn, qr, kcmp, vcmp, kv, gates):
    B, S, W = qn.shape
    tq = TQ_ATT
    n_cmp = (S - CMP_LEN) // CMP_STRIDE + 1
    n_slc = S // SLC_LEN
    c0 = np.arange(LANE)[:, None] * CMP_STRIDE
    s0 = np.arange(LANE)[None, :] * SLC_LEN
    ovl = np.clip(np.minimum(c0 + CMP_LEN, s0 + SLC_LEN) - np.maximum(c0, s0), 0, None) / CMP_LEN
    ovl = ovl * (np.arange(LANE)[:, None] < n_cmp) * (np.arange(LANE)[None, :] < n_slc)
    expand = (np.arange(S)[None, :] // SLC_LEN) == np.arange(LANE)[:, None]
    cmp_spec = pl.BlockSpec((1,) + kcmp.shape[1:], lambda b, q: (b, 0, 0, 0))
    return pl.pallas_call(
        _nsa_kernel,
        grid=(B, S // tq),
        in_specs=[pl.BlockSpec((1, tq, W), lambda b, q: (b, q, 0)),
                  pl.BlockSpec((1, tq, W), lambda b, q: (b, q, 0)),
                  cmp_spec, cmp_spec,
                  pl.BlockSpec((1, S, kv.shape[2]), lambda b, q: (b, 0, 0)),
                  pl.BlockSpec((1, tq, LANE), lambda b, q: (b, q, 0)),
                  pl.BlockSpec((LANE, LANE), lambda b, q: (0, 0)),
                  pl.BlockSpec((LANE, S), lambda b, q: (0, 0))],
        out_specs=pl.BlockSpec((1, tq, W), lambda b, q: (b, q, 0)),
        out_shape=jax.ShapeDtypeStruct((B, S, W), BF16),
        scratch_shapes=[pltpu.VMEM((tq, S), F32)],
        compiler_params=_params("arbitrary", "arbitrary"),
    )(qn, qr, kcmp, vcmp, kv, gates, jnp.asarray(ovl, F32), jnp.asarray(expand, BF16))


def _diff_kernel(qt_ref, k_ref, vt_ref, lam_ref, go_ref, ot_ref):
    t = TQ_DIFF
    qi = pl.program_id(1)
    scale = DIFF_QK_DIM ** -0.5
    lq = lam_ref[...]
    lam = (jnp.exp(jnp.sum(lq[0:1] * lq[1:2], axis=-1, keepdims=True))
           - jnp.exp(jnp.sum(lq[2:3] * lq[3:4], axis=-1, keepdims=True)) + LAMBDA_INIT)
    diag_bias = jnp.where(lax.broadcasted_iota(I32, (t, 1), 0) <= lax.broadcasted_iota(I32, (1, t), 1),
                          0.0, NEG)
    row = lax.broadcasted_iota(I32, (LANE, 1), 0)

    for h in range(DIFF_HEADS):
        tile = (h * 2 * DIFF_QK_DIM) // LANE * LANE
        q128 = qt_ref[0, tile:tile + LANE, :]
        qm = []
        for mp in range(2):
            off = h * 2 * DIFF_QK_DIM + mp * DIFF_QK_DIM - tile
            qm.append(jnp.where((row >= off) & (row < off + DIFF_QK_DIM), q128, jnp.zeros_like(q128)))

        def chunk(k0, carry, bias):
            k = k_ref[0, pl.ds(k0, t), tile:tile + LANE]
            vt = vt_ref[0, h * DIFF_V_DIM:(h + 1) * DIFF_V_DIM, pl.ds(k0, t)]
            new = []
            for mp in range(2):
                m, l, acc = carry[mp]
                s = _dot(k, qm[mp])
                if bias is not None:
                    s = s + bias
                m_new = jnp.maximum(m, jnp.max(s, axis=0, keepdims=True))
                alpha = jnp.exp((m - m_new) * scale)
                p = jnp.exp(s * scale - m_new * scale)
                l = alpha * l + jnp.sum(p, axis=0, keepdims=True)
                acc = alpha * acc + _dot(vt, p.astype(BF16))
                new.append((m_new, l, acc))
            return tuple(new)

        one = (jnp.full((1, t), NEG, F32), jnp.zeros((1, t), F32), jnp.zeros((DIFF_V_DIM, t), F32))
        carry = lax.fori_loop(0, qi, lambda c, cr: chunk(pl.multiple_of(c * t, t), cr, None), (one, one))
        (_, l0, a0), (_, l1, a1) = chunk(pl.multiple_of(qi * t, t), carry, diag_bias)
        o = a0 / l0 - lam * (a1 / l1)
        o = o * lax.rsqrt(jnp.mean(o * o, axis=0, keepdims=True) + EPS) * go_ref[...] * (1.0 - LAMBDA_INIT)
        ot_ref[0, h * DIFF_V_DIM:(h + 1) * DIFF_V_DIM, :] = o.astype(BF16)


def _diff(dq, dkv, lam4, g_out):
    B, S, W = dq.shape
    t = TQ_DIFF
    qt = dq.transpose(0, 2, 1)
    vt = dkv[:, :, W:].transpose(0, 2, 1)
    ot = pl.pallas_call(
        _diff_kernel,
        grid=(B, S // t),
        in_specs=[pl.BlockSpec((1, W, t), lambda b, q: (b, 0, q)),
                  pl.BlockSpec((1, S, W), lambda b, q: (b, 0, 0)),
                  pl.BlockSpec((1, W, S), lambda b, q: (b, 0, 0)),
                  pl.BlockSpec(lam4.shape, lambda b, q: (0, 0)),
                  pl.BlockSpec((DIFF_V_DIM, 1), lambda b, q: (0, 0))],
        out_specs=pl.BlockSpec((1, W, t), lambda b, q: (b, 0, q)),
        out_shape=jax.ShapeDtypeStruct((B, W, S), BF16),
        compiler_params=_params("arbitrary", "arbitrary"),
    )(qt, dkv, vt, lam4, g_out.reshape(DIFF_V_DIM, 1))
    return ot.transpose(0, 2, 1)


def _outproj_kernel(x_ref, on_ref, od_ref, wo_ref, mod_ref, gffn_ref, wr_ref, br_ref,
                    x1_ref, h_ref, route_ref):
    half = on_ref.shape[2]
    gate1 = mod_ref[0, 2:3, :]
    shift2, scale2 = mod_ref[0, 3:4, :], mod_ref[0, 4:5, :]
    attn = _dot(on_ref[0], wo_ref[0:half, :]) + _dot(od_ref[0], wo_ref[half:2 * half, :])
    x1 = x_ref[0] + gate1 * attn
    x1_ref[0] = x1
    ms = jnp.mean(x1 * x1, axis=-1, keepdims=True)
    h = x1 * lax.rsqrt(ms + EPS) * gffn_ref[...] * (1.0 + scale2) + shift2
    h_ref[0] = h

    r = _dot(h, wr_ref[...], precision=HIGHEST) + br_ref[...]
    lg, le = r[:, :LANE], r[:, LANE:]
    lane = lax.broadcasted_iota(I32, (1, LANE), 1).astype(F32)
    lg = jnp.where(lane < N_EXPERT_GROUPS, lg, SENTINEL)
    mg = jnp.max(lg, axis=-1, keepdims=True)
    p_grp = 1.0 / jnp.sum(jnp.exp(lg - mg), axis=-1, keepdims=True)
    grp = jnp.min(jnp.where(lg == mg, lane, float(LANE)), axis=-1, keepdims=True)
    lo = grp * EXPERTS_PER_GROUP
    le = jnp.where((lane >= lo) & (lane < lo + EXPERTS_PER_GROUP), le, SENTINEL)
    v1 = jnp.max(le, axis=-1, keepdims=True)
    i1 = jnp.min(jnp.where(le == v1, lane, float(LANE)), axis=-1, keepdims=True)
    le = jnp.where(lane == i1, SENTINEL, le)
    v2 = jnp.max(le, axis=-1, keepdims=True)
    i2 = jnp.min(jnp.where(le == v2, lane, float(LANE)), axis=-1, keepdims=True)
    e2 = jnp.exp(v2 - v1)
    w1 = p_grp / (1.0 + e2)
    w2 = p_grp * e2 / (1.0 + e2)
    route_ref[0] = jnp.where(lane == 0, i1, jnp.where(lane == 1, i2,
                             jnp.where(lane == 2, w1, jnp.where(lane == 3, w2, 0.0))))


def _outproj(x, o_nsa, o_diff, w_out, mod3, g_ffn, w_rg, b_rg, w_re, b_re):
    B, S, D = x.shape
    tm = TM_PROJ
    half = o_nsa.shape[2]
    wr = jnp.zeros((D, 2 * LANE), F32).at[:, :N_EXPERT_GROUPS].set(w_rg)
    wr = wr.at[:, LANE:LANE + N_EXPERTS].set(w_re)
    br = jnp.zeros((1, 2 * LANE), F32).at[0, :N_EXPERT_GROUPS].set(b_rg)
    br = br.at[0, LANE:LANE + N_EXPERTS].set(b_re)

    def tok(width):
        return pl.BlockSpec((1, tm, width), lambda b, s: (b, s, 0))

    return pl.pallas_call(
        _outproj_kernel,
        grid=(B, S // tm),
        in_specs=[tok(D), tok(half), tok(half),
                  pl.BlockSpec((D, D), lambda b, s: (0, 0)),
                  pl.BlockSpec((1, 6, D), lambda b, s: (b, 0, 0)),
                  pl.BlockSpec((1, D), lambda b, s: (0, 0)),
                  pl.BlockSpec((D, 2 * LANE), lambda b, s: (0, 0)),
                  pl.BlockSpec((1, 2 * LANE), lambda b, s: (0, 0))],
        out_specs=[tok(D), tok(D), tok(LANE)],
        out_shape=[jax.ShapeDtypeStruct((B, S, D), F32), jax.ShapeDtypeStruct((B, S, D), F32),
                   jax.ShapeDtypeStruct((B, S, LANE), F32)],
        compiler_params=_params("arbitrary", "arbitrary"),
    )(x, o_nsa, o_diff, w_out.astype(BF16), mod3, g_ffn.reshape(1, D), wr, br)


def _row_copy(src_hbm, dst, row, slot, sem):
    return pltpu.make_async_copy(src_hbm.at[pl.ds(row, 1)], dst.at[pl.ds(slot, 1)], sem)


def _prefetch_rows(i, n_steps, n_rows, idx_hbm, idx_smem, idx_sem, src_hbm, rows, row_sem):
    slot = i % 2

    def idx_copy(step, s):
        return pltpu.make_async_copy(idx_hbm.at[step], idx_smem.at[s], idx_sem.at[s])

    def issue_rows(s):
        def body(r, carry):
            _row_copy(src_hbm, rows.at[s], idx_smem[s, r // LANE, r % LANE], r, row_sem.at[s]).start()
            return carry
        lax.fori_loop(0, n_rows, body, 0, unroll=8)

    @pl.when(i == 0)
    def _():
        idx_copy(0, 0).start()
        idx_copy(0, 0).wait()
        issue_rows(0)

        @pl.when(n_steps > 1)
        def _():
            idx_copy(1, 1).start()

    @pl.when(i + 1 < n_steps)
    def _():
        idx_copy(i + 1, 1 - slot).wait()
        issue_rows(1 - slot)

        @pl.when(i + 2 < n_steps)
        def _():
            idx_copy(i + 2, slot).start()


def _wait_rows(n_rows, src_hbm, dst, sem):
    pltpu.make_async_copy(src_hbm.at[pl.ds(0, n_rows)], dst, sem).wait()


def _expert_kernel(blk_e_ref, n_used_ref, idx_hbm, h_hbm, wg_ref, wu_ref, wd_ref, o_ref,
                   idx_smem, xbuf, idx_sem, row_sem):
    i = pl.program_id(0)
    n_used = n_used_ref[0]
    slot = i % 2
    _prefetch_rows(i, n_used, MOE_BLOCK, idx_hbm, idx_smem, idx_sem, h_hbm, xbuf, row_sem)

    @pl.when(i < n_used)
    def _():
        _wait_rows(MOE_BLOCK, h_hbm, xbuf.at[slot], row_sem.at[slot])
        xb = xbuf[slot].astype(BF16)
        a = _dot(xb, wg_ref[0])
        u = _dot(xb, wu_ref[0])
        act = a * (1.0 / (1.0 + jnp.exp(-a))) * u
        o_ref[...] = _dot(act.astype(BF16), wd_ref[0])

    @pl.when(i >= n_used)
    def _():
        o_ref[...] = jnp.zeros_like(o_ref)


def _experts(blk_e, n_used, row_src, h, w_g, w_u, w_d):
    bm = MOE_BLOCK
    n_blk = row_src.shape[0]
    D = h.shape[1]
    H = w_g.shape[2]
    grid_spec = pltpu.PrefetchScalarGridSpec(
        num_scalar_prefetch=2,
        grid=(n_blk,),
        in_specs=[pl.BlockSpec(memory_space=pl.ANY), pl.BlockSpec(memory_space=pl.ANY),
                  pl.BlockSpec((1, D, H), lambda i, be, nu: (be[i], 0, 0)),
                  pl.BlockSpec((1, D, H), lambda i, be, nu: (be[i], 0, 0)),
                  pl.BlockSpec((1, H, D), lambda i, be, nu: (be[i], 0, 0))],
        out_specs=pl.BlockSpec((bm, D), lambda i, be, nu: (i, 0)),
        scratch_shapes=[pltpu.SMEM((2, SUBLANE, LANE), I32), pltpu.VMEM((2, bm, D), h.dtype),
                        pltpu.SemaphoreType.DMA((2,)), pltpu.SemaphoreType.DMA((2,))])
    return pl.pallas_call(
        _expert_kernel,
        grid_spec=grid_spec,
        out_shape=jax.ShapeDtypeStruct((n_blk * bm, D), F32),
        compiler_params=_params("arbitrary"),
    )(blk_e, n_used, row_src, h, w_g.astype(BF16), w_u.astype(BF16), w_d.astype(BF16))


def _combine_kernel(pos_hbm, yb_hbm, x1_ref, route_ref, mod_ref, o_ref, idx_smem, rows_ref, idx_sem, row_sem):
    i = pl.program_id(0)
    slot = i % 2
    _prefetch_rows(i, pl.num_programs(0), DMA_ROWS, pos_hbm, idx_smem, idx_sem, yb_hbm, rows_ref, row_sem)
    _wait_rows(DMA_ROWS, yb_hbm, rows_ref.at[slot], row_sem.at[slot])
    tm = DMA_ROWS // 2
    gate2 = mod_ref[0, 5:6, :]
    y = rows_ref[slot, 0:tm, :] * route_ref[:, 2:3] + rows_ref[slot, tm:2 * tm, :] * route_ref[:, 3:4]
    o_ref[...] = x1_ref[...] + gate2 * y


def _combine(pos, yb, x1, route, mod3, seq):
    T, D = x1.shape
    tm = DMA_ROWS // 2
    per_seq = seq // tm
    return pl.pallas_call(
        _combine_kernel,
        grid=(T // tm,),
        in_specs=[pl.BlockSpec(memory_space=pl.ANY), pl.BlockSpec(memory_space=pl.ANY),
                  pl.BlockSpec((tm, D), lambda i: (i, 0)),
                  pl.BlockSpec((tm, LANE), lambda i: (i, 0)),
                  pl.BlockSpec((1, 6, D), lambda i: (i // per_seq, 0, 0))],
        out_specs=pl.BlockSpec((tm, D), lambda i: (i, 0)),
        out_shape=jax.ShapeDtypeStruct((T, D), F32),
        scratch_shapes=[pltpu.SMEM((2, SUBLANE, LANE), I32), pltpu.VMEM((2, DMA_ROWS, D), F32),
                        pltpu.SemaphoreType.DMA((2,)), pltpu.SemaphoreType.DMA((2,))],
        compiler_params=_params("arbitrary"),
    )(pos, yb, x1, route, mod3)


def _dispatch_plan(route, T):
    bm = MOE_BLOCK
    e_flat = route[:, :2].astype(I32).reshape(-1)
    onehot = (e_flat[:, None] == jnp.arange(N_EXPERTS, dtype=I32)[None, :]).astype(I32)
    csum = jnp.cumsum(onehot, axis=0)
    rank = jnp.take_along_axis(csum, e_flat[:, None], axis=1)[:, 0] - 1
    counts = csum[-1]
    padded = (counts + bm - 1) // bm * bm
    pend = jnp.cumsum(padded)
    dest = (pend - padded)[e_flat] + rank
    n_blk = (2 * T + bm - 1) // bm + N_EXPERTS
    n_blk = (n_blk * bm + DMA_ROWS - 1) // DMA_ROWS * DMA_ROWS // bm
    row_src = jnp.zeros((n_blk * bm,), I32).at[dest].set(jnp.arange(2 * T, dtype=I32) // 2)
    blk_start = jnp.arange(n_blk, dtype=I32) * bm
    blk_e = jnp.minimum(jnp.sum((pend[None, :] <= blk_start[:, None]).astype(I32), axis=1), N_EXPERTS - 1)
    n_used = (pend[-1:] // bm).astype(I32)
    return dest, row_src, blk_e, n_used


def kernel(x, c, w_ada, b_ada, g_norm_mix, g_norm_ffn, w_in, g_nsa_q, g_nsa_k, pe_cmp, w_cmp1, w_cmp2, g_diff_q, g_diff_k, lam_q1, lam_k1, lam_q2, lam_k2, g_diff_out, w_out, w_router_group, b_router_group, w_router_expert, b_router_expert, w_exp_gate, w_exp_up, w_exp_down):
    B, S, D = x.shape
    T = B * S
    assert w_ada.shape[0] == 1, "single-layer operation"
    assert S % TK_ATT == 0 and S % TM_PROJ == 0 and S % (DMA_ROWS // 2) == 0

    mod3 = _ada(c, w_ada[0], b_ada[0]).reshape(B, 6, D)
    qn, qr, kcv, kv, gates, dq, dkv = _inproj(
        x, mod3, g_norm_mix[0], w_in[0], g_nsa_q[0], g_nsa_k[0], g_diff_q[0], g_diff_k[0])
    kcmp, vcmp = _compress(kcv, pe_cmp[0], w_cmp1[0], w_cmp2[0], g_nsa_k[0, 0])
    o_nsa = _nsa(qn, qr, kcmp, vcmp, kv, gates)
    lam4 = jnp.stack([lam_q1[0], lam_k1[0], lam_q2[0], lam_k2[0]])
    o_diff = _diff(dq, dkv, lam4, g_diff_out[0])
    x1, h2, route = _outproj(x, o_nsa, o_diff, w_out[0], mod3, g_norm_ffn[0],
                             w_router_group[0], b_router_group[0], w_router_expert[0], b_router_expert[0])

    route = route.reshape(T, LANE)
    dest, row_src, blk_e, n_used = _dispatch_plan(route, T)
    idx_rows = MOE_BLOCK // LANE
    row_src = jnp.pad(row_src.reshape(-1, idx_rows, LANE), ((0, 0), (0, SUBLANE - idx_rows), (0, 0)))
    yb = _experts(blk_e, n_used, row_src, h2.reshape(T, D), w_exp_gate[0], w_exp_up[0], w_exp_down[0])
    tm = DMA_ROWS // 2
    pos = dest.reshape(T // tm, tm, 2).transpose(0, 2, 1).reshape(T // tm, SUBLANE, LANE)
    out = _combine(pos, yb, x1.reshape(T, D), route, mod3, S)
    return out.reshape(B, S, D)
```

```python
import functools
import math

import numpy as np
import jax
import jax.numpy as jnp
from jax import lax
from jax.experimental import pallas as pl
from jax.experimental.pallas import tpu as pltpu

F32, BF16, I32 = jnp.float32, jnp.bfloat16, jnp.int32
HIGHEST = lax.Precision.HIGHEST

HEAD_DIM = 64
NSA_HEADS = 8
NSA_KV_HEADS = 2
NSA_GROUP = NSA_HEADS // NSA_KV_HEADS
CMP_LEN = 32
CMP_STRIDE = 16
CMP_HIDDEN = 256
SLC_LEN = 64
SLC_TOPK = 6
WIN = 256
DIFF_QK_DIM = 32
DIFF_V_DIM = 64
DIFF_HEADS = 8
ROPE_THETA = 500000.0
N_EXPERT_GROUPS = 4
EXPERTS_PER_GROUP = 8
N_EXPERTS = N_EXPERT_GROUPS * EXPERTS_PER_GROUP
EPS = 1e-6
NEG = -1e30
FORCE_BONUS = 1e4
LAMBDA_INIT = 0.8 - 0.6 * math.exp(-0.3 * 0)

LANE = 128
SUBLANE = 8
VMEM_LIMIT = 48 * 1024 * 1024

TM_PROJ = 512
TQ_ATT = 128
TK_ATT = 512
TQ_DIFF = 256
DIFF_HEADS_PER_PASS = 4
MOE_BLOCK = 512
DMA_ROWS = SUBLANE * LANE
SENTINEL = -3e38


def _dot(a, b, **kw):
    return jnp.dot(a, b, preferred_element_type=F32, **kw)


def _dot_nt(a, b):
    return lax.dot_general(a, b, (((1,), (1,)), ((), ())), preferred_element_type=F32)


def _params(*sem):
    return pltpu.CompilerParams(dimension_semantics=sem, vmem_limit_bytes=VMEM_LIMIT)


def _ada_kernel(c_ref, w_ref, b_ref, o_ref):
    o_ref[...] = _dot(c_ref[...], w_ref[...], precision=HIGHEST) + b_ref[...]


def _ada(c, w, b):
    B, D = c.shape
    n = w.shape[1]
    return pl.pallas_call(
        _ada_kernel,
        grid=(n // D,),
        in_specs=[pl.BlockSpec((B, D), lambda j: (0, 0)),
                  pl.BlockSpec((D, D), lambda j: (0, j)),
                  pl.BlockSpec((1, D), lambda j: (0, j))],
        out_specs=pl.BlockSpec((B, D), lambda j: (0, j)),
        out_shape=jax.ShapeDtypeStruct((B, n), F32),
        compiler_params=_params("arbitrary"),
    )(c, w, b.reshape(1, n))


_C_Q, _C_KCV, _C_KS, _C_KW, _C_DQ, _C_DK, _C_DV, _C_GATE, _C_END = (
    0, 512, 768, 1024, 1280, 1792, 2304, 2816, 2944)


def _seg_rms(y, bd, inv_n):
    y2 = y * y
    hi = y2.astype(BF16)
    lo = (y2 - hi.astype(F32)).astype(BF16)
    ss = _dot(hi, bd) + _dot(lo, bd)
    return y * lax.rsqrt(ss * inv_n + EPS)


def _rope(y, c, sa, sb, half):
    return y * c + pltpu.roll(y, LANE - half, 1) * sa + pltpu.roll(y, half, 1) * sb


def _inproj_kernel(x_ref, mod_ref, gmix_ref, w_ref, gains_ref, bd_ref, rope_ref,
                   qn_ref, qr_ref, kcv_ref, kv_ref, gate_ref, dq_ref, dkv_ref):
    x = x_ref[0]
    shift, scale = mod_ref[0, 0:1, :], mod_ref[0, 1:2, :]
    ms = jnp.mean(x * x, axis=-1, keepdims=True)
    h = x * lax.rsqrt(ms + EPS) * gmix_ref[...] * (1.0 + scale) + shift
    hb = h.astype(BF16)
    bd64, bd32 = bd_ref[0], bd_ref[1]
    cn, san, sbn = rope_ref[0], rope_ref[1], rope_ref[2]
    cd, sad, sbd = rope_ref[3], rope_ref[4], rope_ref[5]
    half_n, half_d = HEAD_DIM // 8, DIFF_QK_DIM // 8

    def mm(c0):
        y = _dot(hb, w_ref[:, c0:c0 + 2 * LANE])
        return y[:, :LANE], y[:, LANE:]

    for c in range(2):
        for j, y in enumerate(mm(_C_Q + c * 2 * LANE)):
            col = (2 * c + j) * LANE
            yn = _seg_rms(y, bd64, 1.0 / HEAD_DIM) * gains_ref[0:1, :]
            qn_ref[0, :, col:col + LANE] = yn.astype(BF16)
            qr_ref[0, :, col:col + LANE] = _rope(yn, cn, san, sbn, half_n).astype(BF16)

    kc, vc = mm(_C_KCV)
    kcv_ref[0, :, 0:LANE] = kc.astype(BF16)
    kcv_ref[0, :, LANE:2 * LANE] = vc.astype(BF16)
    for i, c0 in enumerate((_C_KS, _C_KW)):
        k, v = mm(c0)
        kn = _seg_rms(k, bd64, 1.0 / HEAD_DIM) * gains_ref[1 + i:2 + i, :]
        kv_ref[0, :, 2 * i * LANE:(2 * i + 1) * LANE] = _rope(kn, cn, san, sbn, half_n).astype(BF16)
        kv_ref[0, :, (2 * i + 1) * LANE:(2 * i + 2) * LANE] = v.astype(BF16)

    for i, (c0, out_ref) in enumerate(((_C_DQ, dq_ref), (_C_DK, dkv_ref))):
        for c in range(2):
            for j, y in enumerate(mm(c0 + c * 2 * LANE)):
                col = (2 * c + j) * LANE
                yn = _seg_rms(y, bd32, 1.0 / DIFF_QK_DIM) * gains_ref[3 + i:4 + i, :]
                out_ref[0, :, col:col + LANE] = _rope(yn, cd, sad, sbd, half_d).astype(BF16)
    for c in range(2):
        for j, y in enumerate(mm(_C_DV + c * 2 * LANE)):
            col = 4 * LANE + (2 * c + j) * LANE
            dkv_ref[0, :, col:col + LANE] = y.astype(BF16)

    g = _dot(hb, w_ref[:, _C_GATE:_C_END])
    gate_ref[0] = 1.0 / (1.0 + jnp.exp(-g))


def _rope_tables(seq, head_dim):
    rot = head_dim // 4
    half = rot // 2
    inv = ROPE_THETA ** (-jnp.arange(0, rot, 2, dtype=F32) / rot)
    ang = jnp.arange(seq, dtype=F32)[:, None] * inv[None, :]
    cos, sin = jnp.cos(ang), jnp.sin(ang)
    pad = jnp.zeros((seq, head_dim - 2 * half), F32)
    c = jnp.concatenate([cos, cos, pad + 1.0], axis=-1)
    sa = jnp.concatenate([-sin, jnp.zeros_like(sin), pad], axis=-1)
    sb = jnp.concatenate([jnp.zeros_like(sin), sin, pad], axis=-1)
    reps = LANE // head_dim
    return jnp.stack([jnp.tile(t, (1, reps)) for t in (c, sa, sb)])


def _block_diag_ones(seg):
    i = np.arange(LANE)
    return jnp.asarray((i[:, None] // seg) == (i[None, :] // seg), dtype=BF16)


def _inproj(x, mod3, g_mix, w_in, g_nsa_q, g_nsa_k, g_diff_q, g_diff_k):
    B, S, D = x.shape
    tm = TM_PROJ
    gate_w = w_in[:, 1280:1304]
    w = jnp.concatenate(
        [w_in[:, :1280], w_in[:, 1304:], jnp.pad(gate_w, ((0, 0), (0, LANE - gate_w.shape[1])))],
        axis=1).astype(BF16)
    scale = HEAD_DIM ** -0.5
    gains = jnp.stack([
        jnp.tile(g_nsa_q, LANE // HEAD_DIM) * scale,
        jnp.tile(g_nsa_k[1], LANE // HEAD_DIM), jnp.tile(g_nsa_k[2], LANE // HEAD_DIM),
        jnp.tile(g_diff_q, LANE // DIFF_QK_DIM), jnp.tile(g_diff_k, LANE // DIFF_QK_DIM),
        jnp.zeros(LANE), jnp.zeros(LANE), jnp.zeros(LANE)]).astype(F32)
    bd = jnp.stack([_block_diag_ones(HEAD_DIM), _block_diag_ones(DIFF_QK_DIM)])
    rope = jnp.concatenate([_rope_tables(S, HEAD_DIM), _rope_tables(S, DIFF_QK_DIM)])

    def tok(width):
        return pl.BlockSpec((1, tm, width), lambda s, b: (b, s, 0))

    def out(width, dt=BF16):
        return jax.ShapeDtypeStruct((B, S, width), dt)

    return pl.pallas_call(
        _inproj_kernel,
        grid=(S // tm, B),
        in_specs=[tok(D),
                  pl.BlockSpec((1, 6, D), lambda s, b: (b, 0, 0)),
                  pl.BlockSpec((1, D), lambda s, b: (0, 0)),
                  pl.BlockSpec((D, _C_END), lambda s, b: (0, 0)),
                  pl.BlockSpec((SUBLANE, LANE), lambda s, b: (0, 0)),
                  pl.BlockSpec((2, LANE, LANE), lambda s, b: (0, 0, 0)),
                  pl.BlockSpec((6, tm, LANE), lambda s, b: (0, s, 0))],
        out_specs=[tok(512), tok(512), tok(256), tok(512), tok(LANE), tok(512), tok(1024)],
        out_shape=[out(512), out(512), out(256), out(512), out(LANE, F32), out(512), out(1024)],
        compiler_params=_params("arbitrary", "arbitrary"),
    )(x, mod3, g_mix.reshape(1, D), w, gains, bd, rope)


def _gelu_tanh(x):
    return 0.5 * x * (1.0 + jnp.tanh(math.sqrt(2.0 / math.pi) * (x + 0.044715 * (x * x * x))))


def _compress_kernel(r_ref, w1_ref, pe_ref, w2_ref, gk_ref, kc_ref, vc_ref):
    half = CMP_STRIDE * HEAD_DIM
    n_rows = r_ref.shape[2]
    for kv, out_ref in enumerate((kc_ref, vc_ref)):
        r = r_ref[0, kv]
        a = _dot(r, w1_ref[kv, 0:half, :])
        b = _dot(r, w1_ref[kv, half:2 * half, :])
        pb = _dot(pe_ref[kv], w1_ref[kv])
        hid = a + pltpu.roll(b, n_rows - 1, 0) + (pb[0:1] + pb[1:2])
        o = _dot(_gelu_tanh(hid).astype(BF16), w2_ref[kv])
        if kv == 0:
            o = o * lax.rsqrt(jnp.mean(o * o, axis=-1, keepdims=True) + EPS) * gk_ref[...]
        for g in range(NSA_KV_HEADS):
            out_ref[0, g] = o[g * LANE:(g + 1) * LANE].astype(BF16)


def _compress(kcv, pe_cmp, w_cmp1, w_cmp2, g_k0):
    B, S, _ = kcv.shape
    n_rows = S // CMP_STRIDE
    width = CMP_STRIDE * HEAD_DIM
    r = kcv.reshape(B, n_rows, CMP_STRIDE, 2, NSA_KV_HEADS, HEAD_DIM)
    r = r.transpose(0, 3, 4, 1, 2, 5).reshape(B, 2, NSA_KV_HEADS * n_rows, width)
    pe = pe_cmp.reshape(2, 1, CMP_LEN * HEAD_DIM)
    pe_hi = pe.astype(BF16)
    pe_lo = (pe - pe_hi.astype(F32)).astype(BF16)
    pe2 = jnp.concatenate([pe_hi, pe_lo, jnp.zeros((2, SUBLANE - 2, CMP_LEN * HEAD_DIM), BF16)], axis=1)
    shape = jax.ShapeDtypeStruct((B, NSA_KV_HEADS, n_rows, HEAD_DIM), BF16)
    spec = pl.BlockSpec((1, NSA_KV_HEADS, n_rows, HEAD_DIM), lambda b: (b, 0, 0, 0))
    return pl.pallas_call(
        _compress_kernel,
        grid=(B,),
        in_specs=[pl.BlockSpec((1, 2, NSA_KV_HEADS * n_rows, width), lambda b: (b, 0, 0, 0)),
                  pl.BlockSpec((2, 2 * width, CMP_HIDDEN), lambda b: (0, 0, 0)),
                  pl.BlockSpec((2, SUBLANE, 2 * width), lambda b: (0, 0, 0)),
                  pl.BlockSpec((2, CMP_HIDDEN, HEAD_DIM), lambda b: (0, 0, 0)),
                  pl.BlockSpec((1, HEAD_DIM), lambda b: (0, 0))],
        out_specs=[spec, spec],
        out_shape=[shape, shape],
        compiler_params=_params("arbitrary"),
    )(r, w_cmp1.astype(BF16), pe2, w_cmp2.astype(BF16), g_k0.reshape(1, HEAD_DIM))


def _softmax_rows(s):
    m = jnp.max(s, axis=-1, keepdims=True)
    e = jnp.exp(s - m)
    return e / jnp.sum(e, axis=-1, keepdims=True)


def _nsa_kernel(qn_ref, qr_ref, kc_ref, vc_ref, kv_ref, gate_ref, ovl_ref, exp_ref,
                o_ref, selbias_ref, part_ref):
    tq, tk = TQ_ATT, TK_ATT
    qi = pl.program_id(1)
    t_col = qi * tq + lax.broadcasted_iota(I32, (tq, 1), 0)
    lane_i = lax.broadcasted_iota(I32, (1, LANE), 1)
    lane_f = lane_i.astype(F32)
    n_cmp = (kv_ref.shape[1] - CMP_LEN) // CMP_STRIDE + 1
    n_slc = kv_ref.shape[1] // SLC_LEN
    cmp_ok = (lane_i * CMP_STRIDE + (CMP_LEN - 1) <= t_col) & (lane_i < n_cmp)
    cmp_any = (t_col >= CMP_LEN - 1).astype(F32)
    cur = t_col // SLC_LEN
    blk_valid = (lane_i <= cur) & (lane_i < n_slc)
    forced = (lane_i == 0) | (lane_i == cur) | (lane_i == cur - 1)
    n_chunks = (qi * tq + tq + tk - 1) // tk
    w_start = pl.multiple_of(jnp.maximum(qi * tq - WIN, 0), tq)
    w_len = WIN + tq
    w_pos = w_start + lax.broadcasted_iota(I32, (1, w_len), 1)
    w_bias = jnp.where((w_pos <= t_col) & (w_pos > t_col - WIN), 0.0, NEG)
    w_bias4 = jnp.concatenate([w_bias] * NSA_GROUP, axis=0)

    def gate(h, j):
        return gate_ref[0, :, 3 * h + j:3 * h + j + 1]

    def head_cols(h):
        return slice(h * HEAD_DIM, (h + 1) * HEAD_DIM)

    q4 = []
    for g in range(NSA_KV_HEADS):
        heads = [g * NSA_GROUP + n for n in range(NSA_GROUP)]
        kc, vc = kc_ref[0, g], vc_ref[0, g]
        psum = jnp.zeros((tq, LANE), F32)
        o_cmp = []
        for h in heads:
            s = _dot_nt(qn_ref[0, :, head_cols(h)], kc)
            p = _softmax_rows(jnp.where(cmp_ok, s, NEG)) * cmp_any
            psum = psum + p
            o_cmp.append(_dot(p.astype(BF16), vc))
        imp = _dot(psum, ovl_ref[...], precision=HIGHEST)
        score = jnp.where(blk_valid, imp + jnp.where(forced, FORCE_BONUS, 0.0), SENTINEL)
        sel = jnp.zeros((tq, LANE), F32)
        for _ in range(SLC_TOPK):
            m = jnp.max(score, axis=-1, keepdims=True)
            first = jnp.min(jnp.where(score == m, lane_f, float(LANE)), axis=-1, keepdims=True)
            pick = (lane_f == first) & (m > SENTINEL)
            sel = jnp.where(pick, 1.0, sel)
            score = jnp.where(pick, SENTINEL, score)
        selbias_ref[g] = (_dot(sel.astype(BF16), exp_ref[...]) - 1.0) * (-NEG)

        q4.append(jnp.concatenate([qr_ref[0, :, head_cols(h)] for h in heads], axis=0))

        kw = kv_ref[0, pl.ds(w_start, w_len), 2 * LANE + g * HEAD_DIM:2 * LANE + (g + 1) * HEAD_DIM]
        vw = kv_ref[0, pl.ds(w_start, w_len), 3 * LANE + g * HEAD_DIM:3 * LANE + (g + 1) * HEAD_DIM]
        pw = _softmax_rows(_dot_nt(q4[g], kw) + w_bias4)
        o_win = _dot(pw.astype(BF16), vw)
        for n, h in enumerate(heads):
            part_ref[:, head_cols(h)] = gate(h, 0) * o_cmp[n] + gate(h, 2) * o_win[n * tq:(n + 1) * tq]

    def slc_step(c, carry):
        k0 = pl.multiple_of(c * tk, tk)
        kpos = k0 + lax.broadcasted_iota(I32, (1, tk), 1)
        causal = jnp.where(kpos <= t_col, 0.0, NEG)
        new = []
        for g in range(NSA_KV_HEADS):
            m, l, acc = carry[g]
            k = kv_ref[0, pl.ds(k0, tk), g * HEAD_DIM:(g + 1) * HEAD_DIM]
            v = kv_ref[0, pl.ds(k0, tk), LANE + g * HEAD_DIM:LANE + (g + 1) * HEAD_DIM]
            bias = selbias_ref[g, :, pl.ds(k0, tk)] + causal
            s = _dot_nt(q4[g], k) + jnp.concatenate([bias] * NSA_GROUP, axis=0)
            m_new = jnp.maximum(m, jnp.max(s, axis=-1, keepdims=True))
            alpha = jnp.exp(m - m_new)
            p = jnp.exp(s - m_new)
            l = alpha * l + jnp.sum(p, axis=-1, keepdims=True)
            acc = alpha * acc + _dot(p.astype(BF16), v)
            new.append((m_new, l, acc))
        return tuple(new)

    one = (jnp.full((NSA_GROUP * tq, 1), NEG, F32), jnp.zeros((NSA_GROUP * tq, 1), F32),
           jnp.zeros((NSA_GROUP * tq, HEAD_DIM), F32))
    carry = lax.fori_loop(0, n_chunks, slc_step, (one,) * NSA_KV_HEADS)
    for g in range(NSA_KV_HEADS):
        _, l, acc = carry[g]
        o_slc = acc / l
        for n in range(NSA_GROUP):
            h = g * NSA_GROUP + n
            o = part_ref[:, head_cols(h)] + gate(h, 1) * o_slc[n * tq:(n + 1) * tq]
            o_ref[0, :, head_cols(h)] = o.astype(BF16)


def _nsa(qn, qr, kcmp, vcmp, kv, gates):
    B, S, W = qn.shape
    tq = TQ_ATT
    n_cmp = (S - CMP_LEN) // CMP_STRIDE + 1
    n_slc = S // SLC_LEN
    c0 = np.arange(LANE)[:, None] * CMP_STRIDE
    s0 = np.arange(LANE)[None, :] * SLC_LEN
    ovl = np.clip(np.minimum(c0 + CMP_LEN, s0 + SLC_LEN) - np.maximum(c0, s0), 0, None) / CMP_LEN
    ovl = ovl * (np.arange(LANE)[:, None] < n_cmp) * (np.arange(LANE)[None, :] < n_slc)
    expand = (np.arange(S)[None, :] // SLC_LEN) == np.arange(LANE)[:, None]
    cmp_spec = pl.BlockSpec((1,) + kcmp.shape[1:], lambda b, q: (b, 0, 0, 0))
    return pl.pallas_call(
        _nsa_kernel,
        grid=(B, S // tq),
        in_specs=[pl.BlockSpec((1, tq, W), lambda b, q: (b, q, 0)),
                  pl.BlockSpec((1, tq, W), lambda b, q: (b, q, 0)),
                  cmp_spec, cmp_spec,
                  pl.BlockSpec((1, S, kv.shape[2]), lambda b, q: (b, 0, 0)),
                  pl.BlockSpec((1, tq, LANE), lambda b, q: (b, q, 0)),
                  pl.BlockSpec((LANE, LANE), lambda b, q: (0, 0)),
                  pl.BlockSpec((LANE, S), lambda b, q: (0, 0))],
        out_specs=pl.BlockSpec((1, tq, W), lambda b, q: (b, q, 0)),
        out_shape=jax.ShapeDtypeStruct((B, S, W), BF16),
        scratch_shapes=[pltpu.VMEM((NSA_KV_HEADS, tq, S), F32), pltpu.VMEM((tq, W), F32)],
        compiler_params=_params("arbitrary", "arbitrary"),
    )(qn, qr, kcmp, vcmp, kv, gates, jnp.asarray(ovl, F32), jnp.asarray(expand, BF16))


def _diff_kernel(qt_ref, k_ref, vt_ref, lam_ref, go_ref, ot_ref, qm_ref, m_ref, l_ref, acc_ref):
    t = TQ_DIFF
    qi = pl.program_id(1)
    scale = DIFF_QK_DIM ** -0.5
    lq = lam_ref[...]
    lam = (jnp.exp(jnp.sum(lq[0:1] * lq[1:2], axis=-1, keepdims=True))
           - jnp.exp(jnp.sum(lq[2:3] * lq[3:4], axis=-1, keepdims=True)) + LAMBDA_INIT)
    diag_bias = jnp.where(lax.broadcasted_iota(I32, (t, 1), 0) <= lax.broadcasted_iota(I32, (1, t), 1),
                          0.0, NEG)
    row = lax.broadcasted_iota(I32, (LANE, 1), 0)

    def tile_of(h):
        return (h * 2 * DIFF_QK_DIM) // LANE * LANE

    for h in range(DIFF_HEADS):
        q128 = qt_ref[0, tile_of(h):tile_of(h) + LANE, :]
        for mp in range(2):
            off = h * 2 * DIFF_QK_DIM + mp * DIFF_QK_DIM - tile_of(h)
            qm_ref[2 * h + mp] = jnp.where((row >= off) & (row < off + DIFF_QK_DIM), q128, jnp.zeros_like(q128))
    m_ref[...] = jnp.full(m_ref.shape, NEG, F32)
    l_ref[...] = jnp.zeros(l_ref.shape, F32)
    acc_ref[...] = jnp.zeros(acc_ref.shape, F32)

    for h0 in range(0, DIFF_HEADS, DIFF_HEADS_PER_PASS):
        def step(c, carry):
            k0 = pl.multiple_of(c * t, t)
            bias = diag_bias * (c == qi).astype(F32)
            for h in range(h0, h0 + DIFF_HEADS_PER_PASS):
                k = k_ref[0, pl.ds(k0, t), tile_of(h):tile_of(h) + LANE]
                vt = vt_ref[0, h * DIFF_V_DIM:(h + 1) * DIFF_V_DIM, pl.ds(k0, t)]
                for j in (2 * h, 2 * h + 1):
                    s = _dot(k, qm_ref[j]) + bias
                    m = m_ref[j:j + 1, :]
                    m_new = jnp.maximum(m, jnp.max(s, axis=0, keepdims=True))
                    alpha = jnp.exp((m - m_new) * scale)
                    p = jnp.exp(s * scale - m_new * scale)
                    l_ref[j:j + 1, :] = alpha * l_ref[j:j + 1, :] + jnp.sum(p, axis=0, keepdims=True)
                    acc_ref[j] = alpha * acc_ref[j] + _dot(vt, p.astype(BF16))
                    m_ref[j:j + 1, :] = m_new
            return carry
        lax.fori_loop(0, qi + 1, step, 0)

    for h in range(DIFF_HEADS):
        o = (acc_ref[2 * h] / l_ref[2 * h:2 * h + 1, :]
             - lam * (acc_ref[2 * h + 1] / l_ref[2 * h + 1:2 * h + 2, :]))
        o = o * lax.rsqrt(jnp.mean(o * o, axis=0, keepdims=True) + EPS) * go_ref[...] * (1.0 - LAMBDA_INIT)
        ot_ref[0, h * DIFF_V_DIM:(h + 1) * DIFF_V_DIM, :] = o.astype(BF16)


def _diff(dq, dkv, lam4, g_out):
    B, S, W = dq.shape
    t = TQ_DIFF
    qt = dq.transpose(0, 2, 1)
    vt = dkv[:, :, W:].transpose(0, 2, 1)
    ot = pl.pallas_call(
        _diff_kernel,
        grid=(B, S // t),
        in_specs=[pl.BlockSpec((1, W, t), lambda b, q: (b, 0, q)),
                  pl.BlockSpec((1, S, W), lambda b, q: (b, 0, 0)),
                  pl.BlockSpec((1, W, S), lambda b, q: (b, 0, 0)),
                  pl.BlockSpec(lam4.shape, lambda b, q: (0, 0)),
                  pl.BlockSpec((DIFF_V_DIM, 1), lambda b, q: (0, 0))],
        out_specs=pl.BlockSpec((1, W, t), lambda b, q: (b, 0, q)),
        out_shape=jax.ShapeDtypeStruct((B, W, S), BF16),
        scratch_shapes=[pltpu.VMEM((2 * DIFF_HEADS, LANE, t), BF16),
                        pltpu.VMEM((2 * DIFF_HEADS, t), F32), pltpu.VMEM((2 * DIFF_HEADS, t), F32),
                        pltpu.VMEM((2 * DIFF_HEADS, DIFF_V_DIM, t), F32)],
        compiler_params=_params("arbitrary", "arbitrary"),
    )(qt, dkv, vt, lam4, g_out.reshape(DIFF_V_DIM, 1))
    return ot.transpose(0, 2, 1)


def _outproj_kernel(x_ref, on_ref, od_ref, wo_ref, mod_ref, gffn_ref, wr_ref, br_ref,
                    x1_ref, h_ref, route_ref):
    half = on_ref.shape[2]
    gate1 = mod_ref[0, 2:3, :]
    shift2, scale2 = mod_ref[0, 3:4, :], mod_ref[0, 4:5, :]
    attn = _dot(on_ref[0], wo_ref[0:half, :]) + _dot(od_ref[0], wo_ref[half:2 * half, :])
    x1 = x_ref[0] + gate1 * attn
    x1_ref[0] = x1
    ms = jnp.mean(x1 * x1, axis=-1, keepdims=True)
    h = x1 * lax.rsqrt(ms + EPS) * gffn_ref[...] * (1.0 + scale2) + shift2
    h_ref[0] = h

    r = _dot(h, wr_ref[...], precision=HIGHEST) + br_ref[...]
    lg, le = r[:, :LANE], r[:, LANE:]
    lane = lax.broadcasted_iota(I32, (1, LANE), 1).astype(F32)
    lg = jnp.where(lane < N_EXPERT_GROUPS, lg, SENTINEL)
    mg = jnp.max(lg, axis=-1, keepdims=True)
    p_grp = 1.0 / jnp.sum(jnp.exp(lg - mg), axis=-1, keepdims=True)
    grp = jnp.min(jnp.where(lg == mg, lane, float(LANE)), axis=-1, keepdims=True)
    lo = grp * EXPERTS_PER_GROUP
    le = jnp.where((lane >= lo) & (lane < lo + EXPERTS_PER_GROUP), le, SENTINEL)
    v1 = jnp.max(le, axis=-1, keepdims=True)
    i1 = jnp.min(jnp.where(le == v1, lane, float(LANE)), axis=-1, keepdims=True)
    le = jnp.where(lane == i1, SENTINEL, le)
    v2 = jnp.max(le, axis=-1, keepdims=True)
    i2 = jnp.min(jnp.where(le == v2, lane, float(LANE)), axis=-1, keepdims=True)
    e2 = jnp.exp(v2 - v1)
    w1 = p_grp / (1.0 + e2)
    w2 = p_grp * e2 / (1.0 + e2)
    route_ref[0] = jnp.where(lane == 0, i1, jnp.where(lane == 1, i2,
                             jnp.where(lane == 2, w1, jnp.where(lane == 3, w2, 0.0))))


def _outproj(x, o_nsa, o_diff, w_out, mod3, g_ffn, w_rg, b_rg, w_re, b_re):
    B, S, D = x.shape
    tm = TM_PROJ
    half = o_nsa.shape[2]
    wr = jnp.zeros((D, 2 * LANE), F32).at[:, :N_EXPERT_GROUPS].set(w_rg)
    wr = wr.at[:, LANE:LANE + N_EXPERTS].set(w_re)
    br = jnp.zeros((1, 2 * LANE), F32).at[0, :N_EXPERT_GROUPS].set(b_rg)
    br = br.at[0, LANE:LANE + N_EXPERTS].set(b_re)

    def tok(width):
        return pl.BlockSpec((1, tm, width), lambda b, s: (b, s, 0))

    return pl.pallas_call(
        _outproj_kernel,
        grid=(B, S // tm),
        in_specs=[tok(D), tok(half), tok(half),
                  pl.BlockSpec((D, D), lambda b, s: (0, 0)),
                  pl.BlockSpec((1, 6, D), lambda b, s: (b, 0, 0)),
                  pl.BlockSpec((1, D), lambda b, s: (0, 0)),
                  pl.BlockSpec((D, 2 * LANE), lambda b, s: (0, 0)),
                  pl.BlockSpec((1, 2 * LANE), lambda b, s: (0, 0))],
        out_specs=[tok(D), tok(D), tok(LANE)],
        out_shape=[jax.ShapeDtypeStruct((B, S, D), F32), jax.ShapeDtypeStruct((B, S, D), F32),
                   jax.ShapeDtypeStruct((B, S, LANE), F32)],
        compiler_params=_params("arbitrary", "arbitrary"),
    )(x, o_nsa, o_diff, w_out.astype(BF16), mod3, g_ffn.reshape(1, D), wr, br)


def _row_copy(src_hbm, dst, row, slot, sem):
    return pltpu.make_async_copy(src_hbm.at[pl.ds(row, 1)], dst.at[pl.ds(slot, 1)], sem)


def _prefetch_rows(i, n_steps, n_rows, idx_hbm, idx_smem, idx_sem, src_hbm, rows, row_sem):
    slot = i % 2

    def idx_copy(step, s):
        return pltpu.make_async_copy(idx_hbm.at[step], idx_smem.at[s], idx_sem.at[s])

    def issue_rows(s):
        def body(r, carry):
            _row_copy(src_hbm, rows.at[s], idx_smem[s, r // LANE, r % LANE], r, row_sem.at[s]).start()
            return carry
        lax.fori_loop(0, n_rows, body, 0, unroll=8)

    @pl.when(i == 0)
    def _():
        idx_copy(0, 0).start()
        idx_copy(0, 0).wait()
        issue_rows(0)

        @pl.when(n_steps > 1)
        def _():
            idx_copy(1, 1).start()

    @pl.when(i + 1 < n_steps)
    def _():
        idx_copy(i + 1, 1 - slot).wait()
        issue_rows(1 - slot)

        @pl.when(i + 2 < n_steps)
        def _():
            idx_copy(i + 2, slot).start()


def _wait_rows(n_rows, src_hbm, dst, sem):
    pltpu.make_async_copy(src_hbm.at[pl.ds(0, n_rows)], dst, sem).wait()


def _expert_kernel(blk_e_ref, n_used_ref, idx_hbm, h_hbm, wg_ref, wu_ref, wd_ref, o_ref,
                   idx_smem, xbuf, idx_sem, row_sem):
    i = pl.program_id(0)
    n_used = n_used_ref[0]
    slot = i % 2
    _prefetch_rows(i, n_used, MOE_BLOCK, idx_hbm, idx_smem, idx_sem, h_hbm, xbuf, row_sem)

    @pl.when(i < n_used)
    def _():
        _wait_rows(MOE_BLOCK, h_hbm, xbuf.at[slot], row_sem.at[slot])
        xb = xbuf[slot].astype(BF16)
        a = _dot(xb, wg_ref[0])
        u = _dot(xb, wu_ref[0])
        act = a * (1.0 / (1.0 + jnp.exp(-a))) * u
        o_ref[...] = _dot(act.astype(BF16), wd_ref[0])

    @pl.when(i >= n_used)
    def _():
        o_ref[...] = jnp.zeros_like(o_ref)


def _experts(blk_e, n_used, row_src, h, w_g, w_u, w_d):
    bm = MOE_BLOCK
    n_blk = row_src.shape[0]
    D = h.shape[1]
    H = w_g.shape[2]
    grid_spec = pltpu.PrefetchScalarGridSpec(
        num_scalar_prefetch=2,
        grid=(n_blk,),
        in_specs=[pl.BlockSpec(memory_space=pl.ANY), pl.BlockSpec(memory_space=pl.ANY),
                  pl.BlockSpec((1, D, H), lambda i, be, nu: (be[i], 0, 0)),
                  pl.BlockSpec((1, D, H), lambda i, be, nu: (be[i], 0, 0)),
                  pl.BlockSpec((1, H, D), lambda i, be, nu: (be[i], 0, 0))],
        out_specs=pl.BlockSpec((bm, D), lambda i, be, nu: (i, 0)),
        scratch_shapes=[pltpu.SMEM((2, SUBLANE, LANE), I32), pltpu.VMEM((2, bm, D), h.dtype),
                        pltpu.SemaphoreType.DMA((2,)), pltpu.SemaphoreType.DMA((2,))])
    return pl.pallas_call(
        _expert_kernel,
        grid_spec=grid_spec,
        out_shape=jax.ShapeDtypeStruct((n_blk * bm, D), F32),
        compiler_params=_params("arbitrary"),
    )(blk_e, n_used, row_src, h, w_g.astype(BF16), w_u.astype(BF16), w_d.astype(BF16))


def _combine_kernel(pos_hbm, yb_hbm, x1_ref, route_ref, mod_ref, o_ref, idx_smem, rows_ref, idx_sem, row_sem):
    i = pl.program_id(0)
    slot = i % 2
    _prefetch_rows(i, pl.num_programs(0), DMA_ROWS, pos_hbm, idx_smem, idx_sem, yb_hbm, rows_ref, row_sem)
    _wait_rows(DMA_ROWS, yb_hbm, rows_ref.at[slot], row_sem.at[slot])
    tm = DMA_ROWS // 2
    gate2 = mod_ref[0, 5:6, :]
    y = rows_ref[slot, 0:tm, :] * route_ref[:, 2:3] + rows_ref[slot, tm:2 * tm, :] * route_ref[:, 3:4]
    o_ref[...] = x1_ref[...] + gate2 * y


def _combine(pos, yb, x1, route, mod3, seq):
    T, D = x1.shape
    tm = DMA_ROWS // 2
    per_seq = seq // tm
    return pl.pallas_call(
        _combine_kernel,
        grid=(T // tm,),
        in_specs=[pl.BlockSpec(memory_space=pl.ANY), pl.BlockSpec(memory_space=pl.ANY),
                  pl.BlockSpec((tm, D), lambda i: (i, 0)),
                  pl.BlockSpec((tm, LANE), lambda i: (i, 0)),
                  pl.BlockSpec((1, 6, D), lambda i: (i // per_seq, 0, 0))],
        out_specs=pl.BlockSpec((tm, D), lambda i: (i, 0)),
        out_shape=jax.ShapeDtypeStruct((T, D), F32),
        scratch_shapes=[pltpu.SMEM((2, SUBLANE, LANE), I32), pltpu.VMEM((2, DMA_ROWS, D), F32),
                        pltpu.SemaphoreType.DMA((2,)), pltpu.SemaphoreType.DMA((2,))],
        compiler_params=_params("arbitrary"),
    )(pos, yb, x1, route, mod3)


def _dispatch_plan(route, T):
    bm = MOE_BLOCK
    e_flat = route[:, :2].astype(I32).reshape(-1)
    onehot = (e_flat[:, None] == jnp.arange(N_EXPERTS, dtype=I32)[None, :]).astype(I32)
    csum = jnp.cumsum(onehot, axis=0)
    rank = jnp.take_along_axis(csum, e_flat[:, None], axis=1)[:, 0] - 1
    counts = csum[-1]
    padded = (counts + bm - 1) // bm * bm
    pend = jnp.cumsum(padded)
    dest = (pend - padded)[e_flat] + rank
    n_blk = (2 * T + bm - 1) // bm + N_EXPERTS
    n_blk = (n_blk * bm + DMA_ROWS - 1) // DMA_ROWS * DMA_ROWS // bm
    row_src = jnp.zeros((n_blk * bm,), I32).at[dest].set(jnp.arange(2 * T, dtype=I32) // 2)
    blk_start = jnp.arange(n_blk, dtype=I32) * bm
    blk_e = jnp.minimum(jnp.sum((pend[None, :] <= blk_start[:, None]).astype(I32), axis=1), N_EXPERTS - 1)
    n_used = (pend[-1:] // bm).astype(I32)
    return dest, row_src, blk_e, n_used


def kernel(x, c, w_ada, b_ada, g_norm_mix, g_norm_ffn, w_in, g_nsa_q, g_nsa_k, pe_cmp, w_cmp1, w_cmp2, g_diff_q, g_diff_k, lam_q1, lam_k1, lam_q2, lam_k2, g_diff_out, w_out, w_router_group, b_router_group, w_router_expert, b_router_expert, w_exp_gate, w_exp_up, w_exp_down):
    B, S, D = x.shape
    T = B * S
    assert w_ada.shape[0] == 1, "single-layer operation"
    assert S % TK_ATT == 0 and S % TM_PROJ == 0 and S % (DMA_ROWS // 2) == 0

    mod3 = _ada(c, w_ada[0], b_ada[0]).reshape(B, 6, D)
    qn, qr, kcv, kv, gates, dq, dkv = _inproj(
        x, mod3, g_norm_mix[0], w_in[0], g_nsa_q[0], g_nsa_k[0], g_diff_q[0], g_diff_k[0])
    kcmp, vcmp = _compress(kcv, pe_cmp[0], w_cmp1[0], w_cmp2[0], g_nsa_k[0, 0])
    o_nsa = _nsa(qn, qr, kcmp, vcmp, kv, gates)
    lam4 = jnp.stack([lam_q1[0], lam_k1[0], lam_q2[0], lam_k2[0]])
    o_diff = _diff(dq, dkv, lam4, g_diff_out[0])
    x1, h2, route = _outproj(x, o_nsa, o_diff, w_out[0], mod3, g_norm_ffn[0],
                             w_router_group[0], b_router_group[0], w_router_expert[0], b_router_expert[0])

    route = route.reshape(T, LANE)
    dest, row_src, blk_e, n_used = _dispatch_plan(route, T)
    idx_rows = MOE_BLOCK // LANE
    row_src = jnp.pad(row_src.reshape(-1, idx_rows, LANE), ((0, 0), (0, SUBLANE - idx_rows), (0, 0)))
    yb = _experts(blk_e, n_used, row_src, h2.reshape(T, D), w_exp_gate[0], w_exp_up[0], w_exp_down[0])
    tm = DMA_ROWS // 2
    pos = dest.reshape(T // tm, tm, 2).transpose(0, 2, 1).reshape(T // tm, SUBLANE, LANE)
    out = _combine(pos, yb, x1.reshape(T, D), route, mod3, S)
    return out.reshape(B, S, D)
```

```python
import functools
import math

import numpy as np
import jax
import jax.numpy as jnp
from jax import lax
from jax.experimental import pallas as pl
from jax.experimental.pallas import tpu as pltpu

F32, BF16, I32 = jnp.float32, jnp.bfloat16, jnp.int32
HIGHEST = lax.Precision.HIGHEST

HEAD_DIM = 64
NSA_HEADS = 8
NSA_KV_HEADS = 2
NSA_GROUP = NSA_HEADS // NSA_KV_HEADS
CMP_LEN = 32
CMP_STRIDE = 16
CMP_HIDDEN = 256
SLC_LEN = 64
SLC_TOPK = 6
WIN = 256
DIFF_QK_DIM = 32
DIFF_V_DIM = 64
DIFF_HEADS = 8
ROPE_THETA = 500000.0
N_EXPERT_GROUPS = 4
EXPERTS_PER_GROUP = 8
N_EXPERTS = N_EXPERT_GROUPS * EXPERTS_PER_GROUP
EPS = 1e-6
NEG = -1e30
FORCE_BONUS = 1e4
LAMBDA_INIT = 0.8 - 0.6 * math.exp(-0.3 * 0)

LANE = 128
SUBLANE = 8
VMEM_LIMIT = 48 * 1024 * 1024

TM_PROJ = 512
TQ_ATT = 128
TK_ATT = 512
TQ_DIFF = 256
DIFF_HEADS_PER_PASS = 8
MOE_BLOCK = 512
DMA_ROWS = SUBLANE * LANE
SENTINEL = -3e38


def _dot(a, b, **kw):
    return jnp.dot(a, b, preferred_element_type=F32, **kw)


def _dot_nt(a, b):
    return lax.dot_general(a, b, (((1,), (1,)), ((), ())), preferred_element_type=F32)


def _params(*sem):
    return pltpu.CompilerParams(dimension_semantics=sem, vmem_limit_bytes=VMEM_LIMIT)


def _ada_kernel(c_ref, w_ref, b_ref, o_ref):
    o_ref[...] = _dot(c_ref[...], w_ref[...], precision=HIGHEST) + b_ref[...]


def _ada(c, w, b):
    B, D = c.shape
    n = w.shape[1]
    return pl.pallas_call(
        _ada_kernel,
        grid=(n // D,),
        in_specs=[pl.BlockSpec((B, D), lambda j: (0, 0)),
                  pl.BlockSpec((D, D), lambda j: (0, j)),
                  pl.BlockSpec((1, D), lambda j: (0, j))],
        out_specs=pl.BlockSpec((B, D), lambda j: (0, j)),
        out_shape=jax.ShapeDtypeStruct((B, n), F32),
        compiler_params=_params("arbitrary"),
    )(c, w, b.reshape(1, n))


_C_Q, _C_KCV, _C_KS, _C_KW, _C_DQ, _C_DK, _C_DV, _C_GATE, _C_END = (
    0, 512, 768, 1024, 1280, 1792, 2304, 2816, 2944)


def _seg_rms(y, bd, inv_n):
    y2 = y * y
    hi = y2.astype(BF16)
    lo = (y2 - hi.astype(F32)).astype(BF16)
    ss = _dot(hi, bd) + _dot(lo, bd)
    return y * lax.rsqrt(ss * inv_n + EPS)


def _rope(y, c, sa, sb, half):
    return y * c + pltpu.roll(y, LANE - half, 1) * sa + pltpu.roll(y, half, 1) * sb


def _inproj_kernel(x_ref, mod_ref, gmix_ref, w_ref, gains_ref, bd_ref, rope_ref,
                   qn_ref, qr_ref, kcv_ref, kv_ref, gate_ref, dq_ref, dkv_ref):
    x = x_ref[0]
    shift, scale = mod_ref[0, 0:1, :], mod_ref[0, 1:2, :]
    ms = jnp.mean(x * x, axis=-1, keepdims=True)
    h = x * lax.rsqrt(ms + EPS) * gmix_ref[...] * (1.0 + scale) + shift
    hb = h.astype(BF16)
    bd64, bd32 = bd_ref[0], bd_ref[1]
    cn, san, sbn = rope_ref[0], rope_ref[1], rope_ref[2]
    cd, sad, sbd = rope_ref[3], rope_ref[4], rope_ref[5]
    half_n, half_d = HEAD_DIM // 8, DIFF_QK_DIM // 8

    def mm(c0):
        y = _dot(hb, w_ref[:, c0:c0 + 2 * LANE])
        return y[:, :LANE], y[:, LANE:]

    for c in range(2):
        for j, y in enumerate(mm(_C_Q + c * 2 * LANE)):
            col = (2 * c + j) * LANE
            yn = _seg_rms(y, bd64, 1.0 / HEAD_DIM) * gains_ref[0:1, :]
            qn_ref[0, :, col:col + LANE] = yn.astype(BF16)
            qr_ref[0, :, col:col + LANE] = _rope(yn, cn, san, sbn, half_n).astype(BF16)

    kc, vc = mm(_C_KCV)
    kcv_ref[0, :, 0:LANE] = kc.astype(BF16)
    kcv_ref[0, :, LANE:2 * LANE] = vc.astype(BF16)
    for i, c0 in enumerate((_C_KS, _C_KW)):
        k, v = mm(c0)
        kn = _seg_rms(k, bd64, 1.0 / HEAD_DIM) * gains_ref[1 + i:2 + i, :]
        kv_ref[0, :, 2 * i * LANE:(2 * i + 1) * LANE] = _rope(kn, cn, san, sbn, half_n).astype(BF16)
        kv_ref[0, :, (2 * i + 1) * LANE:(2 * i + 2) * LANE] = v.astype(BF16)

    for i, (c0, out_ref) in enumerate(((_C_DQ, dq_ref), (_C_DK, dkv_ref))):
        for c in range(2):
            for j, y in enumerate(mm(c0 + c * 2 * LANE)):
                col = (2 * c + j) * LANE
                yn = _seg_rms(y, bd32, 1.0 / DIFF_QK_DIM) * gains_ref[3 + i:4 + i, :]
                out_ref[0, :, col:col + LANE] = _rope(yn, cd, sad, sbd, half_d).astype(BF16)
    for c in range(2):
        for j, y in enumerate(mm(_C_DV + c * 2 * LANE)):
            col = 4 * LANE + (2 * c + j) * LANE
            dkv_ref[0, :, col:col + LANE] = y.astype(BF16)

    g = _dot(hb, w_ref[:, _C_GATE:_C_END])
    gate_ref[0] = 1.0 / (1.0 + jnp.exp(-g))


def _rope_tables(seq, head_dim):
    rot = head_dim // 4
    half = rot // 2
    inv = ROPE_THETA ** (-jnp.arange(0, rot, 2, dtype=F32) / rot)
    ang = jnp.arange(seq, dtype=F32)[:, None] * inv[None, :]
    cos, sin = jnp.cos(ang), jnp.sin(ang)
    pad = jnp.zeros((seq, head_dim - 2 * half), F32)
    c = jnp.concatenate([cos, cos, pad + 1.0], axis=-1)
    sa = jnp.concatenate([-sin, jnp.zeros_like(sin), pad], axis=-1)
    sb = jnp.concatenate([jnp.zeros_like(sin), sin, pad], axis=-1)
    reps = LANE // head_dim
    return jnp.stack([jnp.tile(t, (1, reps)) for t in (c, sa, sb)])


def _block_diag_ones(seg):
    i = np.arange(LANE)
    return jnp.asarray((i[:, None] // seg) == (i[None, :] // seg), dtype=BF16)


def _inproj(x, mod3, g_mix, w_in, g_nsa_q, g_nsa_k, g_diff_q, g_diff_k):
    B, S, D = x.shape
    tm = TM_PROJ
    gate_w = w_in[:, 1280:1304]
    w = jnp.concatenate(
        [w_in[:, :1280], w_in[:, 1304:], jnp.pad(gate_w, ((0, 0), (0, LANE - gate_w.shape[1])))],
        axis=1).astype(BF16)
    scale = HEAD_DIM ** -0.5
    gains = jnp.stack([
        jnp.tile(g_nsa_q, LANE // HEAD_DIM) * scale,
        jnp.tile(g_nsa_k[1], LANE // HEAD_DIM), jnp.tile(g_nsa_k[2], LANE // HEAD_DIM),
        jnp.tile(g_diff_q, LANE // DIFF_QK_DIM), jnp.tile(g_diff_k, LANE // DIFF_QK_DIM),
        jnp.zeros(LANE), jnp.zeros(LANE), jnp.zeros(LANE)]).astype(F32)
    bd = jnp.stack([_block_diag_ones(HEAD_DIM), _block_diag_ones(DIFF_QK_DIM)])
    rope = jnp.concatenate([_rope_tables(S, HEAD_DIM), _rope_tables(S, DIFF_QK_DIM)])

    def tok(width):
        return pl.BlockSpec((1, tm, width), lambda s, b: (b, s, 0))

    def out(width, dt=BF16):
        return jax.ShapeDtypeStruct((B, S, width), dt)

    return pl.pallas_call(
        _inproj_kernel,
        grid=(S // tm, B),
        in_specs=[tok(D),
                  pl.BlockSpec((1, 6, D), lambda s, b: (b, 0, 0)),
                  pl.BlockSpec((1, D), lambda s, b: (0, 0)),
                  pl.BlockSpec((D, _C_END), lambda s, b: (0, 0)),
                  pl.BlockSpec((SUBLANE, LANE), lambda s, b: (0, 0)),
                  pl.BlockSpec((2, LANE, LANE), lambda s, b: (0, 0, 0)),
                  pl.BlockSpec((6, tm, LANE), lambda s, b: (0, s, 0))],
        out_specs=[tok(512), tok(512), tok(256), tok(512), tok(LANE), tok(512), tok(1024)],
        out_shape=[out(512), out(512), out(256), out(512), out(LANE, F32), out(512), out(1024)],
        compiler_params=_params("arbitrary", "arbitrary"),
    )(x, mod3, g_mix.reshape(1, D), w, gains, bd, rope)


def _gelu_tanh(x):
    return 0.5 * x * (1.0 + jnp.tanh(math.sqrt(2.0 / math.pi) * (x + 0.044715 * (x * x * x))))


def _compress_kernel(r_ref, w1_ref, pe_ref, w2_ref, gk_ref, kc_ref, vc_ref):
    half = CMP_STRIDE * HEAD_DIM
    n_rows = r_ref.shape[2]
    for kv, out_ref in enumerate((kc_ref, vc_ref)):
        r = r_ref[0, kv]
        a = _dot(r, w1_ref[kv, 0:half, :])
        b = _dot(r, w1_ref[kv, half:2 * half, :])
        pb = _dot(pe_ref[kv], w1_ref[kv])
        hid = a + pltpu.roll(b, n_rows - 1, 0) + (pb[0:1] + pb[1:2])
        o = _dot(_gelu_tanh(hid).astype(BF16), w2_ref[kv])
        if kv == 0:
            o = o * lax.rsqrt(jnp.mean(o * o, axis=-1, keepdims=True) + EPS) * gk_ref[...]
        for g in range(NSA_KV_HEADS):
            out_ref[0, g] = o[g * LANE:(g + 1) * LANE].astype(BF16)


def _compress(kcv, pe_cmp, w_cmp1, w_cmp2, g_k0):
    B, S, _ = kcv.shape
    n_rows = S // CMP_STRIDE
    width = CMP_STRIDE * HEAD_DIM
    r = kcv.reshape(B, n_rows, CMP_STRIDE, 2, NSA_KV_HEADS, HEAD_DIM)
    r = r.transpose(0, 3, 4, 1, 2, 5).reshape(B, 2, NSA_KV_HEADS * n_rows, width)
    pe = pe_cmp.reshape(2, 1, CMP_LEN * HEAD_DIM)
    pe_hi = pe.astype(BF16)
    pe_lo = (pe - pe_hi.astype(F32)).astype(BF16)
    pe2 = jnp.concatenate([pe_hi, pe_lo, jnp.zeros((2, SUBLANE - 2, CMP_LEN * HEAD_DIM), BF16)], axis=1)
    shape = jax.ShapeDtypeStruct((B, NSA_KV_HEADS, n_rows, HEAD_DIM), BF16)
    spec = pl.BlockSpec((1, NSA_KV_HEADS, n_rows, HEAD_DIM), lambda b: (b, 0, 0, 0))
    return pl.pallas_call(
        _compress_kernel,
        grid=(B,),
        in_specs=[pl.BlockSpec((1, 2, NSA_KV_HEADS * n_rows, width), lambda b: (b, 0, 0, 0)),
                  pl.BlockSpec((2, 2 * width, CMP_HIDDEN), lambda b: (0, 0, 0)),
                  pl.BlockSpec((2, SUBLANE, 2 * width), lambda b: (0, 0, 0)),
                  pl.BlockSpec((2, CMP_HIDDEN, HEAD_DIM), lambda b: (0, 0, 0)),
                  pl.BlockSpec((1, HEAD_DIM), lambda b: (0, 0))],
        out_specs=[spec, spec],
        out_shape=[shape, shape],
        compiler_params=_params("arbitrary"),
    )(r, w_cmp1.astype(BF16), pe2, w_cmp2.astype(BF16), g_k0.reshape(1, HEAD_DIM))


def _softmax_rows(s):
    m = jnp.max(s, axis=-1, keepdims=True)
    e = jnp.exp(s - m)
    return e / jnp.sum(e, axis=-1, keepdims=True)


def _nsa_kernel(qn_ref, qr_ref, kc_ref, vc_ref, kv_ref, gate_ref, ovl_ref, exp_ref,
                o_ref, selbias_ref, part_ref):
    tq, tk = TQ_ATT, TK_ATT
    qi = pl.program_id(1)
    t_col = qi * tq + lax.broadcasted_iota(I32, (tq, 1), 0)
    lane_i = lax.broadcasted_iota(I32, (1, LANE), 1)
    lane_f = lane_i.astype(F32)
    n_cmp = (kv_ref.shape[1] - CMP_LEN) // CMP_STRIDE + 1
    n_slc = kv_ref.shape[1] // SLC_LEN
    cmp_ok = (lane_i * CMP_STRIDE + (CMP_LEN - 1) <= t_col) & (lane_i < n_cmp)
    cmp_any = (t_col >= CMP_LEN - 1).astype(F32)
    cur = t_col // SLC_LEN
    blk_valid = (lane_i <= cur) & (lane_i < n_slc)
    forced = (lane_i == 0) | (lane_i == cur) | (lane_i == cur - 1)
    n_chunks = (qi * tq + tq + tk - 1) // tk
    w_start = pl.multiple_of(jnp.maximum(qi * tq - WIN, 0), tq)
    w_len = WIN + tq
    w_pos = w_start + lax.broadcasted_iota(I32, (1, w_len), 1)
    w_bias = jnp.where((w_pos <= t_col) & (w_pos > t_col - WIN), 0.0, NEG)
    w_bias4 = jnp.concatenate([w_bias] * NSA_GROUP, axis=0)

    def gate(h, j):
        return gate_ref[0, :, 3 * h + j:3 * h + j + 1]

    def head_cols(h):
        return slice(h * HEAD_DIM, (h + 1) * HEAD_DIM)

    q4 = []
    for g in range(NSA_KV_HEADS):
        heads = [g * NSA_GROUP + n for n in range(NSA_GROUP)]
        kc, vc = kc_ref[0, g], vc_ref[0, g]
        psum = jnp.zeros((tq, LANE), F32)
        o_cmp = []
        for h in heads:
            s = _dot_nt(qn_ref[0, :, head_cols(h)], kc)
            p = _softmax_rows(jnp.where(cmp_ok, s, NEG)) * cmp_any
            psum = psum + p
            o_cmp.append(_dot(p.astype(BF16), vc))
        imp = _dot(psum, ovl_ref[...], precision=HIGHEST)
        score = jnp.where(blk_valid, imp + jnp.where(forced, FORCE_BONUS, 0.0), SENTINEL)
        sel = jnp.zeros((tq, LANE), F32)
        for _ in range(SLC_TOPK):
            m = jnp.max(score, axis=-1, keepdims=True)
            first = jnp.min(jnp.where(score == m, lane_f, float(LANE)), axis=-1, keepdims=True)
            pick = (lane_f == first) & (m > SENTINEL)
            sel = jnp.where(pick, 1.0, sel)
            score = jnp.where(pick, SENTINEL, score)
        selbias_ref[g] = (_dot(sel.astype(BF16), exp_ref[...]) - 1.0) * (-NEG)

        q4.append(jnp.concatenate([qr_ref[0, :, head_cols(h)] for h in heads], axis=0))

        kw = kv_ref[0, pl.ds(w_start, w_len), 2 * LANE + g * HEAD_DIM:2 * LANE + (g + 1) * HEAD_DIM]
        vw = kv_ref[0, pl.ds(w_start, w_len), 3 * LANE + g * HEAD_DIM:3 * LANE + (g + 1) * HEAD_DIM]
        pw = _softmax_rows(_dot_nt(q4[g], kw) + w_bias4)
        o_win = _dot(pw.astype(BF16), vw)
        for n, h in enumerate(heads):
            part_ref[:, head_cols(h)] = gate(h, 0) * o_cmp[n] + gate(h, 2) * o_win[n * tq:(n + 1) * tq]

    def slc_step(c, carry):
        k0 = pl.multiple_of(c * tk, tk)
        kpos = k0 + lax.broadcasted_iota(I32, (1, tk), 1)
        causal = jnp.where(kpos <= t_col, 0.0, NEG)
        new = []
        for g in range(NSA_KV_HEADS):
            m, l, acc = carry[g]
            k = kv_ref[0, pl.ds(k0, tk), g * HEAD_DIM:(g + 1) * HEAD_DIM]
            v = kv_ref[0, pl.ds(k0, tk), LANE + g * HEAD_DIM:LANE + (g + 1) * HEAD_DIM]
            bias = selbias_ref[g, :, pl.ds(k0, tk)] + causal
            s = _dot_nt(q4[g], k) + jnp.concatenate([bias] * NSA_GROUP, axis=0)
            m_new = jnp.maximum(m, jnp.max(s, axis=-1, keepdims=True))
            alpha = jnp.exp(m - m_new)
            p = jnp.exp(s - m_new)
            l = alpha * l + jnp.sum(p, axis=-1, keepdims=True)
            acc = alpha * acc + _dot(p.astype(BF16), v)
            new.append((m_new, l, acc))
        return tuple(new)

    one = (jnp.full((NSA_GROUP * tq, 1), NEG, F32), jnp.zeros((NSA_GROUP * tq, 1), F32),
           jnp.zeros((NSA_GROUP * tq, HEAD_DIM), F32))
    carry = lax.fori_loop(0, n_chunks, slc_step, (one,) * NSA_KV_HEADS)
    for g in range(NSA_KV_HEADS):
        _, l, acc = carry[g]
        o_slc = acc / l
        for n in range(NSA_GROUP):
            h = g * NSA_GROUP + n
            o = part_ref[:, head_cols(h)] + gate(h, 1) * o_slc[n * tq:(n + 1) * tq]
            o_ref[0, :, head_cols(h)] = o.astype(BF16)


def _nsa(qn, qr, kcmp, vcmp, kv, gates):
    B, S, W = qn.shape
    tq = TQ_ATT
    n_cmp = (S - CMP_LEN) // CMP_STRIDE + 1
    n_slc = S // SLC_LEN
    c0 = np.arange(LANE)[:, None] * CMP_STRIDE
    s0 = np.arange(LANE)[None, :] * SLC_LEN
    ovl = np.clip(np.minimum(c0 + CMP_LEN, s0 + SLC_LEN) - np.maximum(c0, s0), 0, None) / CMP_LEN
    ovl = ovl * (np.arange(LANE)[:, None] < n_cmp) * (np.arange(LANE)[None, :] < n_slc)
    expand = (np.arange(S)[None, :] // SLC_LEN) == np.arange(LANE)[:, None]
    cmp_spec = pl.BlockSpec((1,) + kcmp.shape[1:], lambda b, q: (b, 0, 0, 0))
    return pl.pallas_call(
        _nsa_kernel,
        grid=(B, S // tq),
        in_specs=[pl.BlockSpec((1, tq, W), lambda b, q: (b, q, 0)),
                  pl.BlockSpec((1, tq, W), lambda b, q: (b, q, 0)),
                  cmp_spec, cmp_spec,
                  pl.BlockSpec((1, S, kv.shape[2]), lambda b, q: (b, 0, 0)),
                  pl.BlockSpec((1, tq, LANE), lambda b, q: (b, q, 0)),
                  pl.BlockSpec((LANE, LANE), lambda b, q: (0, 0)),
                  pl.BlockSpec((LANE, S), lambda b, q: (0, 0))],
        out_specs=pl.BlockSpec((1, tq, W), lambda b, q: (b, q, 0)),
        out_shape=jax.ShapeDtypeStruct((B, S, W), BF16),
        scratch_shapes=[pltpu.VMEM((NSA_KV_HEADS, tq, S), F32), pltpu.VMEM((tq, W), F32)],
        compiler_params=_params("arbitrary", "arbitrary"),
    )(qn, qr, kcmp, vcmp, kv, gates, jnp.asarray(ovl, F32), jnp.asarray(expand, BF16))


def _diff_kernel(qt_ref, k_ref, vt_ref, lam_ref, go_ref, ot_ref,
                 qm_ref, m_ref, l_ref, alpha_ref, acc_ref, s_ref, p_ref):
    t = TQ_DIFF
    qi = pl.program_id(1)
    scale = DIFF_QK_DIM ** -0.5
    lq = lam_ref[...]
    lam = (jnp.exp(jnp.sum(lq[0:1] * lq[1:2], axis=-1, keepdims=True))
           - jnp.exp(jnp.sum(lq[2:3] * lq[3:4], axis=-1, keepdims=True)) + LAMBDA_INIT)
    diag_bias = jnp.where(lax.broadcasted_iota(I32, (t, 1), 0) <= lax.broadcasted_iota(I32, (1, t), 1),
                          0.0, NEG)
    row = lax.broadcasted_iota(I32, (LANE, 1), 0)

    def tile_of(h):
        return (h * 2 * DIFF_QK_DIM) // LANE * LANE

    for h in range(DIFF_HEADS):
        q128 = qt_ref[0, tile_of(h):tile_of(h) + LANE, :]
        for mp in range(2):
            off = h * 2 * DIFF_QK_DIM + mp * DIFF_QK_DIM - tile_of(h)
            qm_ref[2 * h + mp] = jnp.where((row >= off) & (row < off + DIFF_QK_DIM), q128, jnp.zeros_like(q128))
    m_ref[...] = jnp.full(m_ref.shape, NEG, F32)
    l_ref[...] = jnp.zeros(l_ref.shape, F32)
    acc_ref[...] = jnp.zeros(acc_ref.shape, F32)

    for h0 in range(0, DIFF_HEADS, DIFF_HEADS_PER_PASS):
        def step(c, carry, bias=None):
            k0 = pl.multiple_of(c * t, t)
            chains = [(h, j) for h in range(h0, h0 + DIFF_HEADS_PER_PASS) for j in (2 * h, 2 * h + 1)]
            for h, j in chains:
                k = k_ref[0, pl.ds(k0, t), tile_of(h):tile_of(h) + LANE]
                s = _dot(k, qm_ref[j])
                s_ref[j - 2 * h0] = s if bias is None else s + bias
            for h, j in chains:
                s = s_ref[j - 2 * h0]
                m = m_ref[j:j + 1, :]
                m_new = jnp.maximum(m, jnp.max(s, axis=0, keepdims=True))
                alpha = jnp.exp((m - m_new) * scale)
                p = jnp.exp(s * scale - m_new * scale)
                l_ref[j:j + 1, :] = alpha * l_ref[j:j + 1, :] + jnp.sum(p, axis=0, keepdims=True)
                p_ref[j - 2 * h0] = p.astype(BF16)
                alpha_ref[j:j + 1, :] = alpha
                m_ref[j:j + 1, :] = m_new
            for h, j in chains:
                vt = vt_ref[0, h * DIFF_V_DIM:(h + 1) * DIFF_V_DIM, pl.ds(k0, t)]
                acc_ref[j] = alpha_ref[j:j + 1, :] * acc_ref[j] + _dot(vt, p_ref[j - 2 * h0])
            return carry
        lax.fori_loop(0, qi, step, 0)
        step(qi, 0, diag_bias)

    for h in range(DIFF_HEADS):
        o = (acc_ref[2 * h] / l_ref[2 * h:2 * h + 1, :]
             - lam * (acc_ref[2 * h + 1] / l_ref[2 * h + 1:2 * h + 2, :]))
        o = o * lax.rsqrt(jnp.mean(o * o, axis=0, keepdims=True) + EPS) * go_ref[...] * (1.0 - LAMBDA_INIT)
        ot_ref[0, h * DIFF_V_DIM:(h + 1) * DIFF_V_DIM, :] = o.astype(BF16)


def _diff(dq, dkv, lam4, g_out):
    B, S, W = dq.shape
    t = TQ_DIFF
    qt = dq.transpose(0, 2, 1)
    vt = dkv[:, :, W:].transpose(0, 2, 1)
    ot = pl.pallas_call(
        _diff_kernel,
        grid=(B, S // t),
        in_specs=[pl.BlockSpec((1, W, t), lambda b, q: (b, 0, q)),
                  pl.BlockSpec((1, S, W), lambda b, q: (b, 0, 0)),
                  pl.BlockSpec((1, W, S), lambda b, q: (b, 0, 0)),
                  pl.BlockSpec(lam4.shape, lambda b, q: (0, 0)),
                  pl.BlockSpec((DIFF_V_DIM, 1), lambda b, q: (0, 0))],
        out_specs=pl.BlockSpec((1, W, t), lambda b, q: (b, 0, q)),
        out_shape=jax.ShapeDtypeStruct((B, W, S), BF16),
        scratch_shapes=[pltpu.VMEM((2 * DIFF_HEADS, LANE, t), BF16),
                        pltpu.VMEM((2 * DIFF_HEADS, t), F32), pltpu.VMEM((2 * DIFF_HEADS, t), F32),
                        pltpu.VMEM((2 * DIFF_HEADS, t), F32),
                        pltpu.VMEM((2 * DIFF_HEADS, DIFF_V_DIM, t), F32),
                        pltpu.VMEM((2 * DIFF_HEADS_PER_PASS, t, t), F32),
                        pltpu.VMEM((2 * DIFF_HEADS_PER_PASS, t, t), BF16)],
        compiler_params=_params("arbitrary", "arbitrary"),
    )(qt, dkv, vt, lam4, g_out.reshape(DIFF_V_DIM, 1))
    return ot.transpose(0, 2, 1)


def _outproj_kernel(x_ref, on_ref, od_ref, wo_ref, mod_ref, gffn_ref, wr_ref, br_ref,
                    x1_ref, h_ref, route_ref):
    half = on_ref.shape[2]
    gate1 = mod_ref[0, 2:3, :]
    shift2, scale2 = mod_ref[0, 3:4, :], mod_ref[0, 4:5, :]
    attn = _dot(on_ref[0], wo_ref[0:half, :]) + _dot(od_ref[0], wo_ref[half:2 * half, :])
    x1 = x_ref[0] + gate1 * attn
    x1_ref[0] = x1
    ms = jnp.mean(x1 * x1, axis=-1, keepdims=True)
    h = x1 * lax.rsqrt(ms + EPS) * gffn_ref[...] * (1.0 + scale2) + shift2
    h_ref[0] = h

    r = _dot(h, wr_ref[...], precision=HIGHEST) + br_ref[...]
    lg, le = r[:, :LANE], r[:, LANE:]
    lane = lax.broadcasted_iota(I32, (1, LANE), 1).astype(F32)
    lg = jnp.where(lane < N_EXPERT_GROUPS, lg, SENTINEL)
    mg = jnp.max(lg, axis=-1, keepdims=True)
    p_grp = 1.0 / jnp.sum(jnp.exp(lg - mg), axis=-1, keepdims=True)
    grp = jnp.min(jnp.where(lg == mg, lane, float(LANE)), axis=-1, keepdims=True)
    lo = grp * EXPERTS_PER_GROUP
    le = jnp.where((lane >= lo) & (lane < lo + EXPERTS_PER_GROUP), le, SENTINEL)
    v1 = jnp.max(le, axis=-1, keepdims=True)
    i1 = jnp.min(jnp.where(le == v1, lane, float(LANE)), axis=-1, keepdims=True)
    le = jnp.where(lane == i1, SENTINEL, le)
    v2 = jnp.max(le, axis=-1, keepdims=True)
    i2 = jnp.min(jnp.where(le == v2, lane, float(LANE)), axis=-1, keepdims=True)
    e2 = jnp.exp(v2 - v1)
    w1 = p_grp / (1.0 + e2)
    w2 = p_grp * e2 / (1.0 + e2)
    route_ref[0] = jnp.where(lane == 0, i1, jnp.where(lane == 1, i2,
                             jnp.where(lane == 2, w1, jnp.where(lane == 3, w2, 0.0))))


def _outproj(x, o_nsa, o_diff, w_out, mod3, g_ffn, w_rg, b_rg, w_re, b_re):
    B, S, D = x.shape
    tm = TM_PROJ
    half = o_nsa.shape[2]
    wr = jnp.zeros((D, 2 * LANE), F32).at[:, :N_EXPERT_GROUPS].set(w_rg)
    wr = wr.at[:, LANE:LANE + N_EXPERTS].set(w_re)
    br = jnp.zeros((1, 2 * LANE), F32).at[0, :N_EXPERT_GROUPS].set(b_rg)
    br = br.at[0, LANE:LANE + N_EXPERTS].set(b_re)

    def tok(width):
        return pl.BlockSpec((1, tm, width), lambda b, s: (b, s, 0))

    return pl.pallas_call(
        _outproj_kernel,
        grid=(B, S // tm),
        in_specs=[tok(D), tok(half), tok(half),
                  pl.BlockSpec((D, D), lambda b, s: (0, 0)),
                  pl.BlockSpec((1, 6, D), lambda b, s: (b, 0, 0)),
                  pl.BlockSpec((1, D), lambda b, s: (0, 0)),
                  pl.BlockSpec((D, 2 * LANE), lambda b, s: (0, 0)),
                  pl.BlockSpec((1, 2 * LANE), lambda b, s: (0, 0))],
        out_specs=[tok(D), tok(D), tok(LANE)],
        out_shape=[jax.ShapeDtypeStruct((B, S, D), F32), jax.ShapeDtypeStruct((B, S, D), F32),
                   jax.ShapeDtypeStruct((B, S, LANE), F32)],
        compiler_params=_params("arbitrary", "arbitrary"),
    )(x, o_nsa, o_diff, w_out.astype(BF16), mod3, g_ffn.reshape(1, D), wr, br)


def _row_copy(src_hbm, dst, row, slot, sem):
    return pltpu.make_async_copy(src_hbm.at[pl.ds(row, 1)], dst.at[pl.ds(slot, 1)], sem)


def _prefetch_rows(i, n_steps, n_rows, idx_hbm, idx_smem, idx_sem, src_hbm, rows, row_sem):
    slot = i % 2

    def idx_copy(step, s):
        return pltpu.make_async_copy(idx_hbm.at[step], idx_smem.at[s], idx_sem.at[s])

    def issue_rows(s):
        def body(r, carry):
            _row_copy(src_hbm, rows.at[s], idx_smem[s, r // LANE, r % LANE], r, row_sem.at[s]).start()
            return carry
        lax.fori_loop(0, n_rows, body, 0, unroll=8)

    @pl.when(i == 0)
    def _():
        idx_copy(0, 0).start()
        idx_copy(0, 0).wait()
        issue_rows(0)

        @pl.when(n_steps > 1)
        def _():
            idx_copy(1, 1).start()

    @pl.when(i + 1 < n_steps)
    def _():
        idx_copy(i + 1, 1 - slot).wait()
        issue_rows(1 - slot)

        @pl.when(i + 2 < n_steps)
        def _():
            idx_copy(i + 2, slot).start()


def _wait_rows(n_rows, src_hbm, dst, sem):
    pltpu.make_async_copy(src_hbm.at[pl.ds(0, n_rows)], dst, sem).wait()


def _expert_kernel(blk_e_ref, n_used_ref, idx_hbm, h_hbm, wg_ref, wu_ref, wd_ref, o_ref,
                   idx_smem, xbuf, idx_sem, row_sem):
    i = pl.program_id(0)
    n_used = n_used_ref[0]
    slot = i % 2
    _prefetch_rows(i, n_used, MOE_BLOCK, idx_hbm, idx_smem, idx_sem, h_hbm, xbuf, row_sem)

    @pl.when(i < n_used)
    def _():
        _wait_rows(MOE_BLOCK, h_hbm, xbuf.at[slot], row_sem.at[slot])
        xb = xbuf[slot].astype(BF16)
        a = _dot(xb, wg_ref[0])
        u = _dot(xb, wu_ref[0])
        act = a * (1.0 / (1.0 + jnp.exp(-a))) * u
        o_ref[...] = _dot(act.astype(BF16), wd_ref[0])

    @pl.when(i >= n_used)
    def _():
        o_ref[...] = jnp.zeros_like(o_ref)


def _experts(blk_e, n_used, row_src, h, w_g, w_u, w_d):
    bm = MOE_BLOCK
    n_blk = row_src.shape[0]
    D = h.shape[1]
    H = w_g.shape[2]
    grid_spec = pltpu.PrefetchScalarGridSpec(
        num_scalar_prefetch=2,
        grid=(n_blk,),
        in_specs=[pl.BlockSpec(memory_space=pl.ANY), pl.BlockSpec(memory_space=pl.ANY),
                  pl.BlockSpec((1, D, H), lambda i, be, nu: (be[i], 0, 0)),
                  pl.BlockSpec((1, D, H), lambda i, be, nu: (be[i], 0, 0)),
                  pl.BlockSpec((1, H, D), lambda i, be, nu: (be[i], 0, 0))],
        out_specs=pl.BlockSpec((bm, D), lambda i, be, nu: (i, 0)),
        scratch_shapes=[pltpu.SMEM((2, SUBLANE, LANE), I32), pltpu.VMEM((2, bm, D), h.dtype),
                        pltpu.SemaphoreType.DMA((2,)), pltpu.SemaphoreType.DMA((2,))])
    return pl.pallas_call(
        _expert_kernel,
        grid_spec=grid_spec,
        out_shape=jax.ShapeDtypeStruct((n_blk * bm, D), F32),
        compiler_params=_params("arbitrary"),
    )(blk_e, n_used, row_src, h, w_g.astype(BF16), w_u.astype(BF16), w_d.astype(BF16))


def _combine_kernel(pos_hbm, yb_hbm, x1_ref, route_ref, mod_ref, o_ref, idx_smem, rows_ref, idx_sem, row_sem):
    i = pl.program_id(0)
    slot = i % 2
    _prefetch_rows(i, pl.num_programs(0), DMA_ROWS, pos_hbm, idx_smem, idx_sem, yb_hbm, rows_ref, row_sem)
    _wait_rows(DMA_ROWS, yb_hbm, rows_ref.at[slot], row_sem.at[slot])
    tm = DMA_ROWS // 2
    gate2 = mod_ref[0, 5:6, :]
    y = rows_ref[slot, 0:tm, :] * route_ref[:, 2:3] + rows_ref[slot, tm:2 * tm, :] * route_ref[:, 3:4]
    o_ref[...] = x1_ref[...] + gate2 * y


def _combine(pos, yb, x1, route, mod3, seq):
    T, D = x1.shape
    tm = DMA_ROWS // 2
    per_seq = seq // tm
    return pl.pallas_call(
        _combine_kernel,
        grid=(T // tm,),
        in_specs=[pl.BlockSpec(memory_space=pl.ANY), pl.BlockSpec(memory_space=pl.ANY),
                  pl.BlockSpec((tm, D), lambda i: (i, 0)),
                  pl.BlockSpec((tm, LANE), lambda i: (i, 0)),
                  pl.BlockSpec((1, 6, D), lambda i: (i // per_seq, 0, 0))],
        out_specs=pl.BlockSpec((tm, D), lambda i: (i, 0)),
        out_shape=jax.ShapeDtypeStruct((T, D), F32),
        scratch_shapes=[pltpu.SMEM((2, SUBLANE, LANE), I32), pltpu.VMEM((2, DMA_ROWS, D), F32),
                        pltpu.SemaphoreType.DMA((2,)), pltpu.SemaphoreType.DMA((2,))],
        compiler_params=_params("arbitrary"),
    )(pos, yb, x1, route, mod3)


def _dispatch_plan(route, T):
    bm = MOE_BLOCK
    e_flat = route[:, :2].astype(I32).reshape(-1)
    onehot = (e_flat[:, None] == jnp.arange(N_EXPERTS, dtype=I32)[None, :]).astype(I32)
    csum = jnp.cumsum(onehot, axis=0)
    rank = jnp.take_along_axis(csum, e_flat[:, None], axis=1)[:, 0] - 1
    counts = csum[-1]
    padded = (counts + bm - 1) // bm * bm
    pend = jnp.cumsum(padded)
    dest = (pend - padded)[e_flat] + rank
    n_blk = (2 * T + bm - 1) // bm + N_EXPERTS
    n_blk = (n_blk * bm + DMA_ROWS - 1) // DMA_ROWS * DMA_ROWS // bm
    row_src = jnp.zeros((n_blk * bm,), I32).at[dest].set(jnp.arange(2 * T, dtype=I32) // 2)
    blk_start = jnp.arange(n_blk, dtype=I32) * bm
    blk_e = jnp.minimum(jnp.sum((pend[None, :] <= blk_start[:, None]).astype(I32), axis=1), N_EXPERTS - 1)
    n_used = (pend[-1:] // bm).astype(I32)
    return dest, row_src, blk_e, n_used


def kernel(x, c, w_ada, b_ada, g_norm_mix, g_norm_ffn, w_in, g_nsa_q, g_nsa_k, pe_cmp, w_cmp1, w_cmp2, g_diff_q, g_diff_k, lam_q1, lam_k1, lam_q2, lam_k2, g_diff_out, w_out, w_router_group, b_router_group, w_router_expert, b_router_expert, w_exp_gate, w_exp_up, w_exp_down):
    B, S, D = x.shape
    T = B * S
    assert w_ada.shape[0] == 1, "single-layer operation"
    assert S % TK_ATT == 0 and S % TM_PROJ == 0 and S % (DMA_ROWS // 2) == 0

    mod3 = _ada(c, w_ada[0], b_ada[0]).reshape(B, 6, D)
    qn, qr, kcv, kv, gates, dq, dkv = _inproj(
        x, mod3, g_norm_mix[0], w_in[0], g_nsa_q[0], g_nsa_k[0], g_diff_q[0], g_diff_k[0])
    kcmp, vcmp = _compress(kcv, pe_cmp[0], w_cmp1[0], w_cmp2[0], g_nsa_k[0, 0])
    o_nsa = _nsa(qn, qr, kcmp, vcmp, kv, gates)
    lam4 = jnp.stack([lam_q1[0], lam_k1[0], lam_q2[0], lam_k2[0]])
    o_diff = _diff(dq, dkv, lam4, g_diff_out[0])
    x1, h2, route = _outproj(x, o_nsa, o_diff, w_out[0], mod3, g_norm_ffn[0],
                             w_router_group[0], b_router_group[0], w_router_expert[0], b_router_expert[0])

    route = route.reshape(T, LANE)
    dest, row_src, blk_e, n_used = _dispatch_plan(route, T)
    idx_rows = MOE_BLOCK // LANE
    row_src = jnp.pad(row_src.reshape(-1, idx_rows, LANE), ((0, 0), (0, SUBLANE - idx_rows), (0, 0)))
    yb = _experts(blk_e, n_used, row_src, h2.reshape(T, D), w_exp_gate[0], w_exp_up[0], w_exp_down[0])
    tm = DMA_ROWS // 2
    pos = dest.reshape(T // tm, tm, 2).transpose(0, 2, 1).reshape(T // tm, SUBLANE, LANE)
    out = _combine(pos, yb, x1.reshape(T, D), route, mod3, S)
    return out.reshape(B, S, D)
```

```python
import functools
import math

import numpy as np
import jax
import jax.numpy as jnp
from jax import lax
from jax.experimental import pallas as pl
from jax.experimental.pallas import tpu as pltpu

F32, BF16, I32 = jnp.float32, jnp.bfloat16, jnp.int32
HIGHEST = lax.Precision.HIGHEST

HEAD_DIM = 64
NSA_HEADS = 8
NSA_KV_HEADS = 2
NSA_GROUP = NSA_HEADS // NSA_KV_HEADS
CMP_LEN = 32
CMP_STRIDE = 16
CMP_HIDDEN = 256
SLC_LEN = 64
SLC_TOPK = 6
WIN = 256
DIFF_QK_DIM = 32
DIFF_V_DIM = 64
DIFF_HEADS = 8
ROPE_THETA = 500000.0
N_EXPERT_GROUPS = 4
EXPERTS_PER_GROUP = 8
N_EXPERTS = N_EXPERT_GROUPS * EXPERTS_PER_GROUP
EPS = 1e-6
NEG = -1e30
FORCE_BONUS = 1e4
LAMBDA_INIT = 0.8 - 0.6 * math.exp(-0.3 * 0)

LANE = 128
SUBLANE = 8
VMEM_LIMIT = 48 * 1024 * 1024

TM_PROJ = 512
TQ_ATT = 128
TK_ATT = 512
TK_NSA = 256
TQ_DIFF = 256
DIFF_HEADS_PER_PASS = 8
MOE_BLOCK = 512
DMA_ROWS = SUBLANE * LANE
SENTINEL = -3e38


def _dot(a, b, **kw):
    return jnp.dot(a, b, preferred_element_type=F32, **kw)


def _dot_nt(a, b):
    return lax.dot_general(a, b, (((1,), (1,)), ((), ())), preferred_element_type=F32)


def _params(*sem):
    return pltpu.CompilerParams(dimension_semantics=sem, vmem_limit_bytes=VMEM_LIMIT)


def _ada_kernel(c_ref, w_ref, b_ref, o_ref):
    o_ref[...] = _dot(c_ref[...], w_ref[...], precision=HIGHEST) + b_ref[...]


def _ada(c, w, b):
    B, D = c.shape
    n = w.shape[1]
    return pl.pallas_call(
        _ada_kernel,
        grid=(n // D,),
        in_specs=[pl.BlockSpec((B, D), lambda j: (0, 0)),
                  pl.BlockSpec((D, D), lambda j: (0, j)),
                  pl.BlockSpec((1, D), lambda j: (0, j))],
        out_specs=pl.BlockSpec((B, D), lambda j: (0, j)),
        out_shape=jax.ShapeDtypeStruct((B, n), F32),
        compiler_params=_params("arbitrary"),
    )(c, w, b.reshape(1, n))


_C_Q, _C_KCV, _C_KS, _C_KW, _C_DQ, _C_DK, _C_DV, _C_GATE, _C_END = (
    0, 512, 768, 1024, 1280, 1792, 2304, 2816, 2944)


def _seg_rms(y, bd, inv_n):
    y2 = y * y
    hi = y2.astype(BF16)
    lo = (y2 - hi.astype(F32)).astype(BF16)
    ss = _dot(hi, bd) + _dot(lo, bd)
    return y * lax.rsqrt(ss * inv_n + EPS)


def _rope(y, c, sa, sb, half):
    return y * c + pltpu.roll(y, LANE - half, 1) * sa + pltpu.roll(y, half, 1) * sb


def _inproj_kernel(x_ref, mod_ref, gmix_ref, w_ref, gains_ref, bd_ref, rope_ref,
                   qn_ref, qr_ref, kcv_ref, kv_ref, gate_ref, dq_ref, dkv_ref):
    x = x_ref[0]
    shift, scale = mod_ref[0, 0:1, :], mod_ref[0, 1:2, :]
    ms = jnp.mean(x * x, axis=-1, keepdims=True)
    h = x * lax.rsqrt(ms + EPS) * gmix_ref[...] * (1.0 + scale) + shift
    hb = h.astype(BF16)
    bd64, bd32 = bd_ref[0], bd_ref[1]
    cn, san, sbn = rope_ref[0], rope_ref[1], rope_ref[2]
    cd, sad, sbd = rope_ref[3], rope_ref[4], rope_ref[5]
    half_n, half_d = HEAD_DIM // 8, DIFF_QK_DIM // 8

    def mm(c0):
        y = _dot(hb, w_ref[:, c0:c0 + 2 * LANE])
        return y[:, :LANE], y[:, LANE:]

    for c in range(2):
        for j, y in enumerate(mm(_C_Q + c * 2 * LANE)):
            col = (2 * c + j) * LANE
            yn = _seg_rms(y, bd64, 1.0 / HEAD_DIM) * gains_ref[0:1, :]
            qn_ref[0, :, col:col + LANE] = yn.astype(BF16)
            qr_ref[0, :, col:col + LANE] = _rope(yn, cn, san, sbn, half_n).astype(BF16)

    kc, vc = mm(_C_KCV)
    kcv_ref[0, :, 0:LANE] = kc.astype(BF16)
    kcv_ref[0, :, LANE:2 * LANE] = vc.astype(BF16)
    for i, c0 in enumerate((_C_KS, _C_KW)):
        k, v = mm(c0)
        kn = _seg_rms(k, bd64, 1.0 / HEAD_DIM) * gains_ref[1 + i:2 + i, :]
        kv_ref[0, :, 2 * i * LANE:(2 * i + 1) * LANE] = _rope(kn, cn, san, sbn, half_n).astype(BF16)
        kv_ref[0, :, (2 * i + 1) * LANE:(2 * i + 2) * LANE] = v.astype(BF16)

    for i, (c0, out_ref) in enumerate(((_C_DQ, dq_ref), (_C_DK, dkv_ref))):
        for c in range(2):
            for j, y in enumerate(mm(c0 + c * 2 * LANE)):
                col = (2 * c + j) * LANE
                yn = _seg_rms(y, bd32, 1.0 / DIFF_QK_DIM) * gains_ref[3 + i:4 + i, :]
                out_ref[0, :, col:col + LANE] = _rope(yn, cd, sad, sbd, half_d).astype(BF16)
    for c in range(2):
        for j, y in enumerate(mm(_C_DV + c * 2 * LANE)):
            col = 4 * LANE + (2 * c + j) * LANE
            dkv_ref[0, :, col:col + LANE] = y.astype(BF16)

    g = _dot(hb, w_ref[:, _C_GATE:_C_END])
    gate_ref[0] = 1.0 / (1.0 + jnp.exp(-g))


def _rope_tables(seq, head_dim):
    rot = head_dim // 4
    half = rot // 2
    inv = ROPE_THETA ** (-jnp.arange(0, rot, 2, dtype=F32) / rot)
    ang = jnp.arange(seq, dtype=F32)[:, None] * inv[None, :]
    cos, sin = jnp.cos(ang), jnp.sin(ang)
    pad = jnp.zeros((seq, head_dim - 2 * half), F32)
    c = jnp.concatenate([cos, cos, pad + 1.0], axis=-1)
    sa = jnp.concatenate([-sin, jnp.zeros_like(sin), pad], axis=-1)
    sb = jnp.concatenate([jnp.zeros_like(sin), sin, pad], axis=-1)
    reps = LANE // head_dim
    return jnp.stack([jnp.tile(t, (1, reps)) for t in (c, sa, sb)])


def _block_diag_ones(seg):
    i = np.arange(LANE)
    return jnp.asarray((i[:, None] // seg) == (i[None, :] // seg), dtype=BF16)


def _inproj(x, mod3, g_mix, w_in, g_nsa_q, g_nsa_k, g_diff_q, g_diff_k):
    B, S, D = x.shape
    tm = TM_PROJ
    gate_w = w_in[:, 1280:1304]
    w = jnp.concatenate(
        [w_in[:, :1280], w_in[:, 1304:], jnp.pad(gate_w, ((0, 0), (0, LANE - gate_w.shape[1])))],
        axis=1).astype(BF16)
    scale = HEAD_DIM ** -0.5
    gains = jnp.stack([
        jnp.tile(g_nsa_q, LANE // HEAD_DIM) * scale,
        jnp.tile(g_nsa_k[1], LANE // HEAD_DIM), jnp.tile(g_nsa_k[2], LANE // HEAD_DIM),
        jnp.tile(g_diff_q, LANE // DIFF_QK_DIM), jnp.tile(g_diff_k, LANE // DIFF_QK_DIM),
        jnp.zeros(LANE), jnp.zeros(LANE), jnp.zeros(LANE)]).astype(F32)
    bd = jnp.stack([_block_diag_ones(HEAD_DIM), _block_diag_ones(DIFF_QK_DIM)])
    rope = jnp.concatenate([_rope_tables(S, HEAD_DIM), _rope_tables(S, DIFF_QK_DIM)])

    def tok(width):
        return pl.BlockSpec((1, tm, width), lambda s, b: (b, s, 0))

    def out(width, dt=BF16):
        return jax.ShapeDtypeStruct((B, S, width), dt)

    return pl.pallas_call(
        _inproj_kernel,
        grid=(S // tm, B),
        in_specs=[tok(D),
                  pl.BlockSpec((1, 6, D), lambda s, b: (b, 0, 0)),
                  pl.BlockSpec((1, D), lambda s, b: (0, 0)),
                  pl.BlockSpec((D, _C_END), lambda s, b: (0, 0)),
                  pl.BlockSpec((SUBLANE, LANE), lambda s, b: (0, 0)),
                  pl.BlockSpec((2, LANE, LANE), lambda s, b: (0, 0, 0)),
                  pl.BlockSpec((6, tm, LANE), lambda s, b: (0, s, 0))],
        out_specs=[tok(512), tok(512), tok(256), tok(512), tok(LANE), tok(512), tok(1024)],
        out_shape=[out(512), out(512), out(256), out(512), out(LANE, F32), out(512), out(1024)],
        compiler_params=_params("arbitrary", "arbitrary"),
    )(x, mod3, g_mix.reshape(1, D), w, gains, bd, rope)


def _gelu_tanh(x):
    return 0.5 * x * (1.0 + jnp.tanh(math.sqrt(2.0 / math.pi) * (x + 0.044715 * (x * x * x))))


def _compress_kernel(r_ref, w1_ref, pe_ref, w2_ref, gk_ref, kc_ref, vc_ref):
    half = CMP_STRIDE * HEAD_DIM
    n_rows = r_ref.shape[2]
    for kv, out_ref in enumerate((kc_ref, vc_ref)):
        r = r_ref[0, kv]
        a = _dot(r, w1_ref[kv, 0:half, :])
        b = _dot(r, w1_ref[kv, half:2 * half, :])
        pb = _dot(pe_ref[kv], w1_ref[kv])
        hid = a + pltpu.roll(b, n_rows - 1, 0) + (pb[0:1] + pb[1:2])
        o = _dot(_gelu_tanh(hid).astype(BF16), w2_ref[kv])
        if kv == 0:
            o = o * lax.rsqrt(jnp.mean(o * o, axis=-1, keepdims=True) + EPS) * gk_ref[...]
        for g in range(NSA_KV_HEADS):
            out_ref[0, g] = o[g * LANE:(g + 1) * LANE].astype(BF16)


def _compress(kcv, pe_cmp, w_cmp1, w_cmp2, g_k0):
    B, S, _ = kcv.shape
    n_rows = S // CMP_STRIDE
    width = CMP_STRIDE * HEAD_DIM
    r = kcv.reshape(B, n_rows, CMP_STRIDE, 2, NSA_KV_HEADS, HEAD_DIM)
    r = r.transpose(0, 3, 4, 1, 2, 5).reshape(B, 2, NSA_KV_HEADS * n_rows, width)
    pe = pe_cmp.reshape(2, 1, CMP_LEN * HEAD_DIM)
    pe_hi = pe.astype(BF16)
    pe_lo = (pe - pe_hi.astype(F32)).astype(BF16)
    pe2 = jnp.concatenate([pe_hi, pe_lo, jnp.zeros((2, SUBLANE - 2, CMP_LEN * HEAD_DIM), BF16)], axis=1)
    shape = jax.ShapeDtypeStruct((B, NSA_KV_HEADS, n_rows, HEAD_DIM), BF16)
    spec = pl.BlockSpec((1, NSA_KV_HEADS, n_rows, HEAD_DIM), lambda b: (b, 0, 0, 0))
    return pl.pallas_call(
        _compress_kernel,
        grid=(B,),
        in_specs=[pl.BlockSpec((1, 2, NSA_KV_HEADS * n_rows, width), lambda b: (b, 0, 0, 0)),
                  pl.BlockSpec((2, 2 * width, CMP_HIDDEN), lambda b: (0, 0, 0)),
                  pl.BlockSpec((2, SUBLANE, 2 * width), lambda b: (0, 0, 0)),
                  pl.BlockSpec((2, CMP_HIDDEN, HEAD_DIM), lambda b: (0, 0, 0)),
                  pl.BlockSpec((1, HEAD_DIM), lambda b: (0, 0))],
        out_specs=[spec, spec],
        out_shape=[shape, shape],
        compiler_params=_params("arbitrary"),
    )(r, w_cmp1.astype(BF16), pe2, w_cmp2.astype(BF16), g_k0.reshape(1, HEAD_DIM))


def _softmax_rows(s):
    m = jnp.max(s, axis=-1, keepdims=True)
    e = jnp.exp(s - m)
    return e / jnp.sum(e, axis=-1, keepdims=True)


def _nsa_kernel(qn_ref, qr_ref, kc_ref, vc_ref, kv_ref, gate_ref, ovl_ref, exp_ref,
                o_ref, selbias_ref, part_ref):
    tq, tk = TQ_ATT, TK_ATT
    qi = pl.program_id(1)
    t_col = qi * tq + lax.broadcasted_iota(I32, (tq, 1), 0)
    lane_i = lax.broadcasted_iota(I32, (1, LANE), 1)
    lane_f = lane_i.astype(F32)
    n_cmp = (kv_ref.shape[1] - CMP_LEN) // CMP_STRIDE + 1
    n_slc = kv_ref.shape[1] // SLC_LEN
    cmp_ok = (lane_i * CMP_STRIDE + (CMP_LEN - 1) <= t_col) & (lane_i < n_cmp)
    cmp_any = (t_col >= CMP_LEN - 1).astype(F32)
    cur = t_col // SLC_LEN
    blk_valid = (lane_i <= cur) & (lane_i < n_slc)
    forced = (lane_i == 0) | (lane_i == cur) | (lane_i == cur - 1)
    n_chunks = (qi * tq + tq + tk - 1) // tk
    w_start = pl.multiple_of(jnp.maximum(qi * tq - WIN, 0), tq)
    w_len = WIN + tq
    w_pos = w_start + lax.broadcasted_iota(I32, (1, w_len), 1)
    w_bias = jnp.where((w_pos <= t_col) & (w_pos > t_col - WIN), 0.0, NEG)
    w_bias4 = jnp.concatenate([w_bias] * NSA_GROUP, axis=0)

    def gate(h, j):
        return gate_ref[0, :, 3 * h + j:3 * h + j + 1]

    def head_cols(h):
        return slice(h * HEAD_DIM, (h + 1) * HEAD_DIM)

    q4 = []
    for g in range(NSA_KV_HEADS):
        heads = [g * NSA_GROUP + n for n in range(NSA_GROUP)]
        kc, vc = kc_ref[0, g], vc_ref[0, g]
        psum = jnp.zeros((tq, LANE), F32)
        o_cmp = []
        for h in heads:
            s = _dot_nt(qn_ref[0, :, head_cols(h)], kc)
            p = _softmax_rows(jnp.where(cmp_ok, s, NEG)) * cmp_any
            psum = psum + p
            o_cmp.append(_dot(p.astype(BF16), vc))
        imp = _dot(psum, ovl_ref[...], precision=HIGHEST)
        score = jnp.where(blk_valid, imp + jnp.where(forced, FORCE_BONUS, 0.0), SENTINEL)
        sel = jnp.zeros((tq, LANE), F32)
        for _ in range(SLC_TOPK):
            m = jnp.max(score, axis=-1, keepdims=True)
            first = jnp.min(jnp.where(score == m, lane_f, float(LANE)), axis=-1, keepdims=True)
            pick = (lane_f == first) & (m > SENTINEL)
            sel = jnp.where(pick, 1.0, sel)
            score = jnp.where(pick, SENTINEL, score)
        selbias_ref[g] = (_dot(sel.astype(BF16), exp_ref[...]) - 1.0) * (-NEG)

        q4.append(jnp.concatenate([qr_ref[0, :, head_cols(h)] for h in heads], axis=0))

        kw = kv_ref[0, pl.ds(w_start, w_len), 2 * LANE + g * HEAD_DIM:2 * LANE + (g + 1) * HEAD_DIM]
        vw = kv_ref[0, pl.ds(w_start, w_len), 3 * LANE + g * HEAD_DIM:3 * LANE + (g + 1) * HEAD_DIM]
        pw = _softmax_rows(_dot_nt(q4[g], kw) + w_bias4)
        o_win = _dot(pw.astype(BF16), vw)
        for n, h in enumerate(heads):
            part_ref[:, head_cols(h)] = gate(h, 0) * o_cmp[n] + gate(h, 2) * o_win[n * tq:(n + 1) * tq]

    def slc_step(c, carry):
        k0 = pl.multiple_of(c * tk, tk)
        kpos = k0 + lax.broadcasted_iota(I32, (1, tk), 1)
        causal = jnp.where(kpos <= t_col, 0.0, NEG)
        new = []
        for g in range(NSA_KV_HEADS):
            m, l, acc = carry[g]
            k = kv_ref[0, pl.ds(k0, tk), g * HEAD_DIM:(g + 1) * HEAD_DIM]
            v = kv_ref[0, pl.ds(k0, tk), LANE + g * HEAD_DIM:LANE + (g + 1) * HEAD_DIM]
            bias = selbias_ref[g, :, pl.ds(k0, tk)] + causal
            s = _dot_nt(q4[g], k) + jnp.concatenate([bias] * NSA_GROUP, axis=0)
            m_new = jnp.maximum(m, jnp.max(s, axis=-1, keepdims=True))
            alpha = jnp.exp(m - m_new)
            p = jnp.exp(s - m_new)
            l = alpha * l + jnp.sum(p, axis=-1, keepdims=True)
            acc = alpha * acc + _dot(p.astype(BF16), v)
            new.append((m_new, l, acc))
        return tuple(new)

    one = (jnp.full((NSA_GROUP * tq, 1), NEG, F32), jnp.zeros((NSA_GROUP * tq, 1), F32),
           jnp.zeros((NSA_GROUP * tq, HEAD_DIM), F32))
    carry = lax.fori_loop(0, n_chunks, slc_step, (one,) * NSA_KV_HEADS)
    for g in range(NSA_KV_HEADS):
        _, l, acc = carry[g]
        o_slc = acc / l
        for n in range(NSA_GROUP):
            h = g * NSA_GROUP + n
            o = part_ref[:, head_cols(h)] + gate(h, 1) * o_slc[n * tq:(n + 1) * tq]
            o_ref[0, :, head_cols(h)] = o.astype(BF16)


def _nsa(qn, qr, kcmp, vcmp, kv, gates):
    B, S, W = qn.shape
    tq = TQ_ATT
    n_cmp = (S - CMP_LEN) // CMP_STRIDE + 1
    n_slc = S // SLC_LEN
    c0 = np.arange(LANE)[:, None] * CMP_STRIDE
    s0 = np.arange(LANE)[None, :] * SLC_LEN
    ovl = np.clip(np.minimum(c0 + CMP_LEN, s0 + SLC_LEN) - np.maximum(c0, s0), 0, None) / CMP_LEN
    ovl = ovl * (np.arange(LANE)[:, None] < n_cmp) * (np.arange(LANE)[None, :] < n_slc)
    expand = (np.arange(S)[None, :] // SLC_LEN) == np.arange(LANE)[:, None]
    cmp_spec = pl.BlockSpec((1,) + kcmp.shape[1:], lambda b, q: (b, 0, 0, 0))
    return pl.pallas_call(
        _nsa_kernel,
        grid=(B, S // tq),
        in_specs=[pl.BlockSpec((1, tq, W), lambda b, q: (b, q, 0)),
                  pl.BlockSpec((1, tq, W), lambda b, q: (b, q, 0)),
                  cmp_spec, cmp_spec,
                  pl.BlockSpec((1, S, kv.shape[2]), lambda b, q: (b, 0, 0)),
                  pl.BlockSpec((1, tq, LANE), lambda b, q: (b, q, 0)),
                  pl.BlockSpec((LANE, LANE), lambda b, q: (0, 0)),
                  pl.BlockSpec((LANE, S), lambda b, q: (0, 0))],
        out_specs=pl.BlockSpec((1, tq, W), lambda b, q: (b, q, 0)),
        out_shape=jax.ShapeDtypeStruct((B, S, W), BF16),
        scratch_shapes=[pltpu.VMEM((NSA_KV_HEADS, tq, S), F32), pltpu.VMEM((tq, W), F32)],
        compiler_params=_params("arbitrary", "arbitrary"),
    )(qn, qr, kcmp, vcmp, kv, gates, jnp.asarray(ovl, F32), jnp.asarray(expand, BF16))


def _softmax_cols(s):
    m = jnp.max(s, axis=0, keepdims=True)
    e = jnp.exp(s - m)
    return e / jnp.sum(e, axis=0, keepdims=True)


def _nsa_t_kernel(qnt_ref, qrt_ref, kc_ref, vct_ref, kv_ref, vt_ref, gt_ref, ovlt_ref, ot_ref,
                  selb_ref, qz_ref, s_ref, p_ref, m_ref, l_ref, alpha_ref, acc_ref, part_ref):
    tq, tk = TQ_ATT, TK_NSA
    qi = pl.program_id(1)
    seq = kv_ref.shape[1]
    n_cmp = (seq - CMP_LEN) // CMP_STRIDE + 1
    n_slc = seq // SLC_LEN
    t_row = qi * tq + lax.broadcasted_iota(I32, (1, tq), 1)
    c_col = lax.broadcasted_iota(I32, (kc_ref.shape[2], 1), 0)
    cmp_bias = jnp.where((c_col * CMP_STRIDE + (CMP_LEN - 1) <= t_row) & (c_col < n_cmp), 0.0, NEG)
    cmp_any = (t_row >= CMP_LEN - 1).astype(F32)
    b_col = lax.broadcasted_iota(I32, (n_slc, 1), 0)
    b_f = b_col.astype(F32)
    cur = t_row // SLC_LEN
    blk_valid = b_col <= cur
    forced = (b_col == 0) | (b_col == cur) | (b_col == cur - 1)
    w_start = pl.multiple_of(jnp.maximum(qi * tq - WIN, 0), tq)
    w_len = WIN + tq
    w_pos = w_start + lax.broadcasted_iota(I32, (w_len, 1), 0)
    w_bias = jnp.where((w_pos <= t_row) & (w_pos > t_row - WIN), 0.0, NEG)

    def tile4(a):
        return jnp.concatenate([a] * NSA_GROUP, axis=1)

    def gate(h, j):
        return gt_ref[0, 3 * h + j:3 * h + j + 1, :]

    def head_rows(h):
        return slice(h * HEAD_DIM, (h + 1) * HEAD_DIM)

    def q_lanes(n):
        return slice(n * tq, (n + 1) * tq)

    zeros = jnp.zeros((HEAD_DIM, NSA_GROUP * tq), BF16)
    for g in range(NSA_KV_HEADS):
        heads = [g * NSA_GROUP + n for n in range(NSA_GROUP)]
        qn4 = jnp.concatenate([qnt_ref[0, head_rows(h), :] for h in heads], axis=1)
        qr4 = jnp.concatenate([qrt_ref[0, head_rows(h), :] for h in heads], axis=1)
        p = _softmax_cols(_dot(kc_ref[0, g], qn4) + tile4(cmp_bias)) * tile4(cmp_any)
        o_cmp = _dot(vct_ref[0, g], p.astype(BF16))
        psum = p[:, q_lanes(0)]
        for n in range(1, NSA_GROUP):
            psum = psum + p[:, q_lanes(n)]
        imp = _dot(ovlt_ref[...], psum, precision=HIGHEST)
        score = jnp.where(blk_valid, imp + jnp.where(forced, FORCE_BONUS, 0.0), SENTINEL)
        sel = jnp.zeros((n_slc, tq), F32)
        for _ in range(SLC_TOPK):
            m = jnp.max(score, axis=0, keepdims=True)
            first = jnp.min(jnp.where(score == m, b_f, float(n_slc)), axis=0, keepdims=True)
            pick = (b_f == first) & (m > SENTINEL)
            sel = jnp.where(pick, 1.0, sel)
            score = jnp.where(pick, SENTINEL, score)
        sel_bias = (sel - 1.0) * (-NEG)
        for j in range(n_slc):
            selb_ref[g, j * SLC_LEN:(j + 1) * SLC_LEN, :] = jnp.broadcast_to(sel_bias[j:j + 1, :], (SLC_LEN, tq))
        qz = jnp.concatenate([qr4, zeros] if g == 0 else [zeros, qr4], axis=0)
        qz_ref[g] = qz
        kw = kv_ref[0, pl.ds(w_start, w_len), 2 * LANE:3 * LANE]
        pw = _softmax_cols(_dot(kw, qz) + tile4(w_bias))
        o_win = _dot(vt_ref[0, LANE + g * HEAD_DIM:LANE + (g + 1) * HEAD_DIM, pl.ds(w_start, w_len)],
                     pw.astype(BF16))
        for n, h in enumerate(heads):
            part_ref[head_rows(h), :] = gate(h, 0) * o_cmp[:, q_lanes(n)] + gate(h, 2) * o_win[:, q_lanes(n)]

    m_ref[...] = jnp.full(m_ref.shape, NEG, F32)
    l_ref[...] = jnp.zeros(l_ref.shape, F32)
    acc_ref[...] = jnp.zeros(acc_ref.shape, F32)

    def step(c, carry, causal=None):
        k0 = pl.multiple_of(c * tk, tk)
        k = kv_ref[0, pl.ds(k0, tk), 0:LANE]
        for g in range(NSA_KV_HEADS):
            bias = selb_ref[g, pl.ds(k0, tk), :]
            if causal is not None:
                bias = bias + causal
            s_ref[g] = _dot(k, qz_ref[g]) + tile4(bias)
        for g in range(NSA_KV_HEADS):
            s = s_ref[g]
            m = m_ref[g:g + 1, :]
            m_new = jnp.maximum(m, jnp.max(s, axis=0, keepdims=True))
            alpha = jnp.exp(m - m_new)
            p = jnp.exp(s - m_new)
            l_ref[g:g + 1, :] = alpha * l_ref[g:g + 1, :] + jnp.sum(p, axis=0, keepdims=True)
            p_ref[g] = p.astype(BF16)
            alpha_ref[g:g + 1, :] = alpha
            m_ref[g:g + 1, :] = m_new
        for g in range(NSA_KV_HEADS):
            vt = vt_ref[0, g * HEAD_DIM:(g + 1) * HEAD_DIM, pl.ds(k0, tk)]
            acc_ref[g] = alpha_ref[g:g + 1, :] * acc_ref[g] + _dot(vt, p_ref[g])
        return carry

    n_full = (qi * tq) // tk
    lax.fori_loop(0, n_full, step, 0)
    k_pos = n_full * tk + lax.broadcasted_iota(I32, (tk, 1), 0)
    step(n_full, 0, jnp.where(k_pos <= t_row, 0.0, NEG))

    for g in range(NSA_KV_HEADS):
        o_slc = acc_ref[g] / l_ref[g:g + 1, :]
        for n in range(NSA_GROUP):
            h = g * NSA_GROUP + n
            o = part_ref[head_rows(h), :] + gate(h, 1) * o_slc[:, q_lanes(n)]
            ot_ref[0, head_rows(h), :] = o.astype(BF16)


def _nsa_t(qn, qr, kcmp, vcmp, kv, gates):
    B, S, W = qn.shape
    tq, tk = TQ_ATT, TK_NSA
    n_cmp = (S - CMP_LEN) // CMP_STRIDE + 1
    n_slc = S // SLC_LEN
    n_rows = kcmp.shape[2]
    c0 = np.arange(n_rows)[None, :] * CMP_STRIDE
    s0 = np.arange(n_slc)[:, None] * SLC_LEN
    ovlt = np.clip(np.minimum(c0 + CMP_LEN, s0 + SLC_LEN) - np.maximum(c0, s0), 0, None) / CMP_LEN
    ovlt = ovlt * (np.arange(n_rows)[None, :] < n_cmp)
    qnt, qrt = qn.transpose(0, 2, 1), qr.transpose(0, 2, 1)
    vt = jnp.concatenate([kv[:, :, LANE:2 * LANE], kv[:, :, 3 * LANE:4 * LANE]], axis=-1).transpose(0, 2, 1)
    gt = gates.transpose(0, 2, 1)
    vct = vcmp.transpose(0, 1, 3, 2)
    wide = NSA_GROUP * tq
    ot = pl.pallas_call(
        _nsa_t_kernel,
        grid=(B, S // tq),
        in_specs=[pl.BlockSpec((1, W, tq), lambda b, q: (b, 0, q)),
                  pl.BlockSpec((1, W, tq), lambda b, q: (b, 0, q)),
                  pl.BlockSpec((1,) + kcmp.shape[1:], lambda b, q: (b, 0, 0, 0)),
                  pl.BlockSpec((1,) + vct.shape[1:], lambda b, q: (b, 0, 0, 0)),
                  pl.BlockSpec((1, S, kv.shape[2]), lambda b, q: (b, 0, 0)),
                  pl.BlockSpec((1, 2 * LANE, S), lambda b, q: (b, 0, 0)),
                  pl.BlockSpec((1, LANE, tq), lambda b, q: (b, 0, q)),
                  pl.BlockSpec((n_slc, n_rows), lambda b, q: (0, 0))],
        out_specs=pl.BlockSpec((1, W, tq), lambda b, q: (b, 0, q)),
        out_shape=jax.ShapeDtypeStruct((B, W, S), BF16),
        scratch_shapes=[pltpu.VMEM((NSA_KV_HEADS, S, tq), F32),
                        pltpu.VMEM((NSA_KV_HEADS, LANE, wide), BF16),
                        pltpu.VMEM((NSA_KV_HEADS, tk, wide), F32),
                        pltpu.VMEM((NSA_KV_HEADS, tk, wide), BF16),
                        pltpu.VMEM((NSA_KV_HEADS, wide), F32), pltpu.VMEM((NSA_KV_HEADS, wide), F32),
                        pltpu.VMEM((NSA_KV_HEADS, wide), F32),
                        pltpu.VMEM((NSA_KV_HEADS, HEAD_DIM, wide), F32),
                        pltpu.VMEM((W, tq), F32)],
        compiler_params=_params("arbitrary", "arbitrary"),
    )(qnt, qrt, kcmp, vct, kv, vt, gt, jnp.asarray(ovlt, F32))
    return ot.transpose(0, 2, 1)


def _diff_kernel(qt_ref, k_ref, vt_ref, lam_ref, go_ref, ot_ref,
                 qm_ref, m_ref, l_ref, alpha_ref, acc_ref, s_ref, p_ref):
    t = TQ_DIFF
    qi = pl.program_id(1)
    scale = DIFF_QK_DIM ** -0.5
    lq = lam_ref[...]
    lam = (jnp.exp(jnp.sum(lq[0:1] * lq[1:2], axis=-1, keepdims=True))
           - jnp.exp(jnp.sum(lq[2:3] * lq[3:4], axis=-1, keepdims=True)) + LAMBDA_INIT)
    diag_bias = jnp.where(lax.broadcasted_iota(I32, (t, 1), 0) <= lax.broadcasted_iota(I32, (1, t), 1),
                          0.0, NEG)
    row = lax.broadcasted_iota(I32, (LANE, 1), 0)

    def tile_of(h):
        return (h * 2 * DIFF_QK_DIM) // LANE * LANE

    for h in range(DIFF_HEADS):
        q128 = qt_ref[0, tile_of(h):tile_of(h) + LANE, :]
        for mp in range(2):
            off = h * 2 * DIFF_QK_DIM + mp * DIFF_QK_DIM - tile_of(h)
            qm_ref[2 * h + mp] = jnp.where((row >= off) & (row < off + DIFF_QK_DIM), q128, jnp.zeros_like(q128))
    m_ref[...] = jnp.full(m_ref.shape, NEG, F32)
    l_ref[...] = jnp.zeros(l_ref.shape, F32)
    acc_ref[...] = jnp.zeros(acc_ref.shape, F32)

    for h0 in range(0, DIFF_HEADS, DIFF_HEADS_PER_PASS):
        def step(c, carry, bias=None):
            k0 = pl.multiple_of(c * t, t)
            chains = [(h, j) for h in range(h0, h0 + DIFF_HEADS_PER_PASS) for j in (2 * h, 2 * h + 1)]
            for h, j in chains:
                k = k_ref[0, pl.ds(k0, t), tile_of(h):tile_of(h) + LANE]
                s = _dot(k, qm_ref[j])
                s_ref[j - 2 * h0] = s if bias is None else s + bias
            for h, j in chains:
                s = s_ref[j - 2 * h0]
                m = m_ref[j:j + 1, :]
                m_new = jnp.maximum(m, jnp.max(s, axis=0, keepdims=True))
                alpha = jnp.exp((m - m_new) * scale)
                p = jnp.exp(s * scale - m_new * scale)
                l_ref[j:j + 1, :] = alpha * l_ref[j:j + 1, :] + jnp.sum(p, axis=0, keepdims=True)
                p_ref[j - 2 * h0] = p.astype(BF16)
                alpha_ref[j:j + 1, :] = alpha
                m_ref[j:j + 1, :] = m_new
            for h, j in chains:
                vt = vt_ref[0, h * DIFF_V_DIM:(h + 1) * DIFF_V_DIM, pl.ds(k0, t)]
                acc_ref[j] = alpha_ref[j:j + 1, :] * acc_ref[j] + _dot(vt, p_ref[j - 2 * h0])
            return carry
        lax.fori_loop(0, qi, step, 0)
        step(qi, 0, diag_bias)

    for h in range(DIFF_HEADS):
        o = (acc_ref[2 * h] / l_ref[2 * h:2 * h + 1, :]
             - lam * (acc_ref[2 * h + 1] / l_ref[2 * h + 1:2 * h + 2, :]))
        o = o * lax.rsqrt(jnp.mean(o * o, axis=0, keepdims=True) + EPS) * go_ref[...] * (1.0 - LAMBDA_INIT)
        ot_ref[0, h * DIFF_V_DIM:(h + 1) * DIFF_V_DIM, :] = o.astype(BF16)


def _diff(dq, dkv, lam4, g_out):
    B, S, W = dq.shape
    t = TQ_DIFF
    qt = dq.transpose(0, 2, 1)
    vt = dkv[:, :, W:].transpose(0, 2, 1)
    ot = pl.pallas_call(
        _diff_kernel,
        grid=(B, S // t),
        in_specs=[pl.BlockSpec((1, W, t), lambda b, q: (b, 0, q)),
                  pl.BlockSpec((1, S, W), lambda b, q: (b, 0, 0)),
                  pl.BlockSpec((1, W, S), lambda b, q: (b, 0, 0)),
                  pl.BlockSpec(lam4.shape, lambda b, q: (0, 0)),
                  pl.BlockSpec((DIFF_V_DIM, 1), lambda b, q: (0, 0))],
        out_specs=pl.BlockSpec((1, W, t), lambda b, q: (b, 0, q)),
        out_shape=jax.ShapeDtypeStruct((B, W, S), BF16),
        scratch_shapes=[pltpu.VMEM((2 * DIFF_HEADS, LANE, t), BF16),
                        pltpu.VMEM((2 * DIFF_HEADS, t), F32), pltpu.VMEM((2 * DIFF_HEADS, t), F32),
                        pltpu.VMEM((2 * DIFF_HEADS, t), F32),
                        pltpu.VMEM((2 * DIFF_HEADS, DIFF_V_DIM, t), F32),
                        pltpu.VMEM((2 * DIFF_HEADS_PER_PASS, t, t), F32),
                        pltpu.VMEM((2 * DIFF_HEADS_PER_PASS, t, t), BF16)],
        compiler_params=_params("arbitrary", "arbitrary"),
    )(qt, dkv, vt, lam4, g_out.reshape(DIFF_V_DIM, 1))
    return ot.transpose(0, 2, 1)


def _outproj_kernel(x_ref, on_ref, od_ref, wo_ref, mod_ref, gffn_ref, wr_ref, br_ref,
                    x1_ref, h_ref, route_ref):
    half = on_ref.shape[2]
    gate1 = mod_ref[0, 2:3, :]
    shift2, scale2 = mod_ref[0, 3:4, :], mod_ref[0, 4:5, :]
    attn = _dot(on_ref[0], wo_ref[0:half, :]) + _dot(od_ref[0], wo_ref[half:2 * half, :])
    x1 = x_ref[0] + gate1 * attn
    x1_ref[0] = x1
    ms = jnp.mean(x1 * x1, axis=-1, keepdims=True)
    h = x1 * lax.rsqrt(ms + EPS) * gffn_ref[...] * (1.0 + scale2) + shift2
    h_ref[0] = h

    r = _dot(h, wr_ref[...], precision=HIGHEST) + br_ref[...]
    lg, le = r[:, :LANE], r[:, LANE:]
    lane = lax.broadcasted_iota(I32, (1, LANE), 1).astype(F32)
    lg = jnp.where(lane < N_EXPERT_GROUPS, lg, SENTINEL)
    mg = jnp.max(lg, axis=-1, keepdims=True)
    p_grp = 1.0 / jnp.sum(jnp.exp(lg - mg), axis=-1, keepdims=True)
    grp = jnp.min(jnp.where(lg == mg, lane, float(LANE)), axis=-1, keepdims=True)
    lo = grp * EXPERTS_PER_GROUP
    le = jnp.where((lane >= lo) & (lane < lo + EXPERTS_PER_GROUP), le, SENTINEL)
    v1 = jnp.max(le, axis=-1, keepdims=True)
    i1 = jnp.min(jnp.where(le == v1, lane, float(LANE)), axis=-1, keepdims=True)
    le = jnp.where(lane == i1, SENTINEL, le)
    v2 = jnp.max(le, axis=-1, keepdims=True)
    i2 = jnp.min(jnp.where(le == v2, lane, float(LANE)), axis=-1, keepdims=True)
    e2 = jnp.exp(v2 - v1)
    w1 = p_grp / (1.0 + e2)
    w2 = p_grp * e2 / (1.0 + e2)
    route_ref[0] = jnp.where(lane == 0, i1, jnp.where(lane == 1, i2,
                             jnp.where(lane == 2, w1, jnp.where(lane == 3, w2, 0.0))))


def _outproj(x, o_nsa, o_diff, w_out, mod3, g_ffn, w_rg, b_rg, w_re, b_re):
    B, S, D = x.shape
    tm = TM_PROJ
    half = o_nsa.shape[2]
    wr = jnp.zeros((D, 2 * LANE), F32).at[:, :N_EXPERT_GROUPS].set(w_rg)
    wr = wr.at[:, LANE:LANE + N_EXPERTS].set(w_re)
    br = jnp.zeros((1, 2 * LANE), F32).at[0, :N_EXPERT_GROUPS].set(b_rg)
    br = br.at[0, LANE:LANE + N_EXPERTS].set(b_re)

    def tok(width):
        return pl.BlockSpec((1, tm, width), lambda b, s: (b, s, 0))

    return pl.pallas_call(
        _outproj_kernel,
        grid=(B, S // tm),
        in_specs=[tok(D), tok(half), tok(half),
                  pl.BlockSpec((D, D), lambda b, s: (0, 0)),
                  pl.BlockSpec((1, 6, D), lambda b, s: (b, 0, 0)),
                  pl.BlockSpec((1, D), lambda b, s: (0, 0)),
                  pl.BlockSpec((D, 2 * LANE), lambda b, s: (0, 0)),
                  pl.BlockSpec((1, 2 * LANE), lambda b, s: (0, 0))],
        out_specs=[tok(D), tok(D), tok(LANE)],
        out_shape=[jax.ShapeDtypeStruct((B, S, D), F32), jax.ShapeDtypeStruct((B, S, D), F32),
                   jax.ShapeDtypeStruct((B, S, LANE), F32)],
        compiler_params=_params("arbitrary", "arbitrary"),
    )(x, o_nsa, o_diff, w_out.astype(BF16), mod3, g_ffn.reshape(1, D), wr, br)


def _row_copy(src_hbm, dst, row, slot, sem):
    return pltpu.make_async_copy(src_hbm.at[pl.ds(row, 1)], dst.at[pl.ds(slot, 1)], sem)


def _prefetch_rows(i, n_steps, n_rows, idx_hbm, idx_smem, idx_sem, src_hbm, rows, row_sem):
    slot = i % 2

    def idx_copy(step, s):
        return pltpu.make_async_copy(idx_hbm.at[step], idx_smem.at[s], idx_sem.at[s])

    def issue_rows(s):
        def body(r, carry):
            _row_copy(src_hbm, rows.at[s], idx_smem[s, r // LANE, r % LANE], r, row_sem.at[s]).start()
            return carry
        lax.fori_loop(0, n_rows, body, 0, unroll=8)

    @pl.when(i == 0)
    def _():
        idx_copy(0, 0).start()
        idx_copy(0, 0).wait()
        issue_rows(0)

        @pl.when(n_steps > 1)
        def _():
            idx_copy(1, 1).start()

    @pl.when(i + 1 < n_steps)
    def _():
        idx_copy(i + 1, 1 - slot).wait()
        issue_rows(1 - slot)

        @pl.when(i + 2 < n_steps)
        def _():
            idx_copy(i + 2, slot).start()


def _wait_rows(n_rows, src_hbm, dst, sem):
    pltpu.make_async_copy(src_hbm.at[pl.ds(0, n_rows)], dst, sem).wait()


def _expert_kernel(blk_e_ref, n_used_ref, idx_hbm, h_hbm, wg_ref, wu_ref, wd_ref, o_ref,
                   idx_smem, xbuf, idx_sem, row_sem):
    i = pl.program_id(0)
    n_used = n_used_ref[0]
    slot = i % 2
    _prefetch_rows(i, n_used, MOE_BLOCK, idx_hbm, idx_smem, idx_sem, h_hbm, xbuf, row_sem)

    @pl.when(i < n_used)
    def _():
        _wait_rows(MOE_BLOCK, h_hbm, xbuf.at[slot], row_sem.at[slot])
        xb = xbuf[slot].astype(BF16)
        a = _dot(xb, wg_ref[0])
        u = _dot(xb, wu_ref[0])
        act = a * (1.0 / (1.0 + jnp.exp(-a))) * u
        o_ref[...] = _dot(act.astype(BF16), wd_ref[0])

    @pl.when(i >= n_used)
    def _():
        o_ref[...] = jnp.zeros_like(o_ref)


def _experts(blk_e, n_used, row_src, h, w_g, w_u, w_d):
    bm = MOE_BLOCK
    n_blk = row_src.shape[0]
    D = h.shape[1]
    H = w_g.shape[2]
    grid_spec = pltpu.PrefetchScalarGridSpec(
        num_scalar_prefetch=2,
        grid=(n_blk,),
        in_specs=[pl.BlockSpec(memory_space=pl.ANY), pl.BlockSpec(memory_space=pl.ANY),
                  pl.BlockSpec((1, D, H), lambda i, be, nu: (be[i], 0, 0)),
                  pl.BlockSpec((1, D, H), lambda i, be, nu: (be[i], 0, 0)),
                  pl.BlockSpec((1, H, D), lambda i, be, nu: (be[i], 0, 0))],
        out_specs=pl.BlockSpec((bm, D), lambda i, be, nu: (i, 0)),
        scratch_shapes=[pltpu.SMEM((2, SUBLANE, LANE), I32), pltpu.VMEM((2, bm, D), h.dtype),
                        pltpu.SemaphoreType.DMA((2,)), pltpu.SemaphoreType.DMA((2,))])
    return pl.pallas_call(
        _expert_kernel,
        grid_spec=grid_spec,
        out_shape=jax.ShapeDtypeStruct((n_blk * bm, D), F32),
        compiler_params=_params("arbitrary"),
    )(blk_e, n_used, row_src, h, w_g.astype(BF16), w_u.astype(BF16), w_d.astype(BF16))


def _combine_kernel(pos_hbm, yb_hbm, x1_ref, route_ref, mod_ref, o_ref, idx_smem, rows_ref, idx_sem, row_sem):
    i = pl.program_id(0)
    slot = i % 2
    _prefetch_rows(i, pl.num_programs(0), DMA_ROWS, pos_hbm, idx_smem, idx_sem, yb_hbm, rows_ref, row_sem)
    _wait_rows(DMA_ROWS, yb_hbm, rows_ref.at[slot], row_sem.at[slot])
    tm = DMA_ROWS // 2
    gate2 = mod_ref[0, 5:6, :]
    y = rows_ref[slot, 0:tm, :] * route_ref[:, 2:3] + rows_ref[slot, tm:2 * tm, :] * route_ref[:, 3:4]
    o_ref[...] = x1_ref[...] + gate2 * y


def _combine(pos, yb, x1, route, mod3, seq):
    T, D = x1.shape
    tm = DMA_ROWS // 2
    per_seq = seq // tm
    return pl.pallas_call(
        _combine_kernel,
        grid=(T // tm,),
        in_specs=[pl.BlockSpec(memory_space=pl.ANY), pl.BlockSpec(memory_space=pl.ANY),
                  pl.BlockSpec((tm, D), lambda i: (i, 0)),
                  pl.BlockSpec((tm, LANE), lambda i: (i, 0)),
                  pl.BlockSpec((1, 6, D), lambda i: (i // per_seq, 0, 0))],
        out_specs=pl.BlockSpec((tm, D), lambda i: (i, 0)),
        out_shape=jax.ShapeDtypeStruct((T, D), F32),
        scratch_shapes=[pltpu.SMEM((2, SUBLANE, LANE), I32), pltpu.VMEM((2, DMA_ROWS, D), F32),
                        pltpu.SemaphoreType.DMA((2,)), pltpu.SemaphoreType.DMA((2,))],
        compiler_params=_params("arbitrary"),
    )(pos, yb, x1, route, mod3)


def _dispatch_plan(route, T):
    bm = MOE_BLOCK
    e_flat = route[:, :2].astype(I32).reshape(-1)
    onehot = (e_flat[:, None] == jnp.arange(N_EXPERTS, dtype=I32)[None, :]).astype(I32)
    csum = jnp.cumsum(onehot, axis=0)
    rank = jnp.take_along_axis(csum, e_flat[:, None], axis=1)[:, 0] - 1
    counts = csum[-1]
    padded = (counts + bm - 1) // bm * bm
    pend = jnp.cumsum(padded)
    dest = (pend - padded)[e_flat] + rank
    n_blk = (2 * T + bm - 1) // bm + N_EXPERTS
    n_blk = (n_blk * bm + DMA_ROWS - 1) // DMA_ROWS * DMA_ROWS // bm
    row_src = jnp.zeros((n_blk * bm,), I32).at[dest].set(jnp.arange(2 * T, dtype=I32) // 2)
    blk_start = jnp.arange(n_blk, dtype=I32) * bm
    blk_e = jnp.minimum(jnp.sum((pend[None, :] <= blk_start[:, None]).astype(I32), axis=1), N_EXPERTS - 1)
    n_used = (pend[-1:] // bm).astype(I32)
    return dest, row_src, blk_e, n_used


def kernel(x, c, w_ada, b_ada, g_norm_mix, g_norm_ffn, w_in, g_nsa_q, g_nsa_k, pe_cmp, w_cmp1, w_cmp2, g_diff_q, g_diff_k, lam_q1, lam_k1, lam_q2, lam_k2, g_diff_out, w_out, w_router_group, b_router_group, w_router_expert, b_router_expert, w_exp_gate, w_exp_up, w_exp_down):
    B, S, D = x.shape
    T = B * S
    assert w_ada.shape[0] == 1, "single-layer operation"
    assert S % TK_ATT == 0 and S % TM_PROJ == 0 and S % (DMA_ROWS // 2) == 0

    mod3 = _ada(c, w_ada[0], b_ada[0]).reshape(B, 6, D)
    qn, qr, kcv, kv, gates, dq, dkv = _inproj(
        x, mod3, g_norm_mix[0], w_in[0], g_nsa_q[0], g_nsa_k[0], g_diff_q[0], g_diff_k[0])
    kcmp, vcmp = _compress(kcv, pe_cmp[0], w_cmp1[0], w_cmp2[0], g_nsa_k[0, 0])
    o_nsa = _nsa_t(qn, qr, kcmp, vcmp, kv, gates)
    lam4 = jnp.stack([lam_q1[0], lam_k1[0], lam_q2[0], lam_k2[0]])
    o_diff = _diff(dq, dkv, lam4, g_diff_out[0])
    x1, h2, route = _outproj(x, o_nsa, o_diff, w_out[0], mod3, g_norm_ffn[0],
                             w_router_group[0], b_router_group[0], w_router_expert[0], b_router_expert[0])

    route = route.reshape(T, LANE)
    dest, row_src, blk_e, n_used = _dispatch_plan(route, T)
    idx_rows = MOE_BLOCK // LANE
    row_src = jnp.pad(row_src.reshape(-1, idx_rows, LANE), ((0, 0), (0, SUBLANE - idx_rows), (0, 0)))
    yb = _experts(blk_e, n_used, row_src, h2.reshape(T, D), w_exp_gate[0], w_exp_up[0], w_exp_down[0])
    tm = DMA_ROWS // 2
    pos = dest.reshape(T // tm, tm, 2).transpose(0, 2, 1).reshape(T // tm, SUBLANE, LANE)
    out = _combine(pos, yb, x1.reshape(T, D), route, mod3, S)
    return out.reshape(B, S, D)
```

```python
import math

import numpy as np
import jax
import jax.numpy as jnp
from jax import lax
from jax.experimental import pallas as pl
from jax.experimental.pallas import tpu as pltpu

F32, BF16, I32 = jnp.float32, jnp.bfloat16, jnp.int32
HIGHEST = lax.Precision.HIGHEST

HEAD_DIM = 64
NSA_HEADS = 8
NSA_KV_HEADS = 2
NSA_GROUP = NSA_HEADS // NSA_KV_HEADS
CMP_LEN = 32
CMP_STRIDE = 16
CMP_HIDDEN = 256
SLC_LEN = 64
SLC_TOPK = 6
WIN = 256
DIFF_QK_DIM = 32
DIFF_V_DIM = 64
DIFF_HEADS = 8
ROPE_THETA = 500000.0
N_EXPERT_GROUPS = 4
EXPERTS_PER_GROUP = 8
N_EXPERTS = N_EXPERT_GROUPS * EXPERTS_PER_GROUP
EPS = 1e-6
NEG = -1e30
FORCE_BONUS = 1e4
LAMBDA_INIT = 0.8 - 0.6 * math.exp(-0.3 * 0)

LANE = 128
SUBLANE = 8
VMEM_LIMIT = 48 * 1024 * 1024

TM_PROJ = 512
TQ_ATT = 128
TK_NSA = 256
TQ_DIFF = 256
DIFF_HEADS_PER_PASS = 8
MOE_BLOCK = 512
DMA_ROWS = SUBLANE * LANE
SENTINEL = -3e38


def _dot(a, b, **kw):
    return jnp.dot(a, b, preferred_element_type=F32, **kw)


def _params(*sem):
    return pltpu.CompilerParams(dimension_semantics=sem, vmem_limit_bytes=VMEM_LIMIT)


def _ada_kernel(c_ref, w_ref, b_ref, o_ref):
    o_ref[...] = _dot(c_ref[...], w_ref[...], precision=HIGHEST) + b_ref[...]


def _ada(c, w, b):
    B, D = c.shape
    n = w.shape[1]
    return pl.pallas_call(
        _ada_kernel,
        grid=(n // D,),
        in_specs=[pl.BlockSpec((B, D), lambda j: (0, 0)),
                  pl.BlockSpec((D, D), lambda j: (0, j)),
                  pl.BlockSpec((1, D), lambda j: (0, j))],
        out_specs=pl.BlockSpec((B, D), lambda j: (0, j)),
        out_shape=jax.ShapeDtypeStruct((B, n), F32),
        compiler_params=_params("arbitrary"),
    )(c, w, b.reshape(1, n))


_C_Q, _C_KCV, _C_KS, _C_KW, _C_DQ, _C_DK, _C_DV, _C_GATE, _C_END = (
    0, 512, 768, 1024, 1280, 1792, 2304, 2816, 2944)


def _seg_rms(y, bd, inv_n):
    y2 = y * y
    hi = y2.astype(BF16)
    lo = (y2 - hi.astype(F32)).astype(BF16)
    ss = _dot(hi, bd) + _dot(lo, bd)
    return y * lax.rsqrt(ss * inv_n + EPS)


def _rope(y, c, sa, sb, half):
    return y * c + pltpu.roll(y, LANE - half, 1) * sa + pltpu.roll(y, half, 1) * sb


def _inproj_kernel(x_ref, mod_ref, gmix_ref, w_ref, gains_ref, bd_ref, rope_ref,
                   qn_ref, qr_ref, kcv_ref, kv_ref, gate_ref, dq_ref, dkv_ref):
    x = x_ref[0]
    shift, scale = mod_ref[0, 0:1, :], mod_ref[0, 1:2, :]
    ms = jnp.mean(x * x, axis=-1, keepdims=True)
    h = x * lax.rsqrt(ms + EPS) * gmix_ref[...] * (1.0 + scale) + shift
    hb = h.astype(BF16)
    bd64, bd32 = bd_ref[0], bd_ref[1]
    cn, san, sbn = rope_ref[0], rope_ref[1], rope_ref[2]
    cd, sad, sbd = rope_ref[3], rope_ref[4], rope_ref[5]
    half_n, half_d = HEAD_DIM // 8, DIFF_QK_DIM // 8

    def mm(c0):
        y = _dot(hb, w_ref[:, c0:c0 + 2 * LANE])
        return y[:, :LANE], y[:, LANE:]

    for c in range(2):
        for j, y in enumerate(mm(_C_Q + c * 2 * LANE)):
            col = (2 * c + j) * LANE
            yn = _seg_rms(y, bd64, 1.0 / HEAD_DIM) * gains_ref[0:1, :]
            qn_ref[0, :, col:col + LANE] = yn.astype(BF16)
            qr_ref[0, :, col:col + LANE] = _rope(yn, cn, san, sbn, half_n).astype(BF16)

    kc, vc = mm(_C_KCV)
    kcv_ref[0, :, 0:LANE] = kc.astype(BF16)
    kcv_ref[0, :, LANE:2 * LANE] = vc.astype(BF16)
    for i, c0 in enumerate((_C_KS, _C_KW)):
        k, v = mm(c0)
        kn = _seg_rms(k, bd64, 1.0 / HEAD_DIM) * gains_ref[1 + i:2 + i, :]
        kv_ref[0, :, 2 * i * LANE:(2 * i + 1) * LANE] = _rope(kn, cn, san, sbn, half_n).astype(BF16)
        kv_ref[0, :, (2 * i + 1) * LANE:(2 * i + 2) * LANE] = v.astype(BF16)

    for i, (c0, out_ref) in enumerate(((_C_DQ, dq_ref), (_C_DK, dkv_ref))):
        for c in range(2):
            for j, y in enumerate(mm(c0 + c * 2 * LANE)):
                col = (2 * c + j) * LANE
                yn = _seg_rms(y, bd32, 1.0 / DIFF_QK_DIM) * gains_ref[3 + i:4 + i, :]
                out_ref[0, :, col:col + LANE] = _rope(yn, cd, sad, sbd, half_d).astype(BF16)
    for c in range(2):
        for j, y in enumerate(mm(_C_DV + c * 2 * LANE)):
            col = 4 * LANE + (2 * c + j) * LANE
            dkv_ref[0, :, col:col + LANE] = y.astype(BF16)

    g = _dot(hb, w_ref[:, _C_GATE:_C_END])
    gate_ref[0] = 1.0 / (1.0 + jnp.exp(-g))


def _rope_tables(seq, head_dim):
    rot = head_dim // 4
    half = rot // 2
    inv = ROPE_THETA ** (-jnp.arange(0, rot, 2, dtype=F32) / rot)
    ang = jnp.arange(seq, dtype=F32)[:, None] * inv[None, :]
    cos, sin = jnp.cos(ang), jnp.sin(ang)
    pad = jnp.zeros((seq, head_dim - 2 * half), F32)
    c = jnp.concatenate([cos, cos, pad + 1.0], axis=-1)
    sa = jnp.concatenate([-sin, jnp.zeros_like(sin), pad], axis=-1)
    sb = jnp.concatenate([jnp.zeros_like(sin), sin, pad], axis=-1)
    reps = LANE // head_dim
    return jnp.stack([jnp.tile(t, (1, reps)) for t in (c, sa, sb)])


def _block_diag_ones(seg):
    i = np.arange(LANE)
    return jnp.asarray((i[:, None] // seg) == (i[None, :] // seg), dtype=BF16)


def _inproj(x, mod3, g_mix, w_in, g_nsa_q, g_nsa_k, g_diff_q, g_diff_k):
    B, S, D = x.shape
    tm = TM_PROJ
    gate_w = w_in[:, 1280:1304]
    w = jnp.concatenate(
        [w_in[:, :1280], w_in[:, 1304:], jnp.pad(gate_w, ((0, 0), (0, LANE - gate_w.shape[1])))],
        axis=1).astype(BF16)
    scale = HEAD_DIM ** -0.5
    gains = jnp.stack([
        jnp.tile(g_nsa_q, LANE // HEAD_DIM) * scale,
        jnp.tile(g_nsa_k[1], LANE // HEAD_DIM), jnp.tile(g_nsa_k[2], LANE // HEAD_DIM),
        jnp.tile(g_diff_q, LANE // DIFF_QK_DIM), jnp.tile(g_diff_k, LANE // DIFF_QK_DIM),
        jnp.zeros(LANE), jnp.zeros(LANE), jnp.zeros(LANE)]).astype(F32)
    bd = jnp.stack([_block_diag_ones(HEAD_DIM), _block_diag_ones(DIFF_QK_DIM)])
    rope = jnp.concatenate([_rope_tables(S, HEAD_DIM), _rope_tables(S, DIFF_QK_DIM)])

    def tok(width):
        return pl.BlockSpec((1, tm, width), lambda s, b: (b, s, 0))

    def out(width, dt=BF16):
        return jax.ShapeDtypeStruct((B, S, width), dt)

    return pl.pallas_call(
        _inproj_kernel,
        grid=(S // tm, B),
        in_specs=[tok(D),
                  pl.BlockSpec((1, 6, D), lambda s, b: (b, 0, 0)),
                  pl.BlockSpec((1, D), lambda s, b: (0, 0)),
                  pl.BlockSpec((D, _C_END), lambda s, b: (0, 0)),
                  pl.BlockSpec((SUBLANE, LANE), lambda s, b: (0, 0)),
                  pl.BlockSpec((2, LANE, LANE), lambda s, b: (0, 0, 0)),
                  pl.BlockSpec((6, tm, LANE), lambda s, b: (0, s, 0))],
        out_specs=[tok(512), tok(512), tok(256), tok(512), tok(LANE), tok(512), tok(1024)],
        out_shape=[out(512), out(512), out(256), out(512), out(LANE, F32), out(512), out(1024)],
        compiler_params=_params("arbitrary", "arbitrary"),
    )(x, mod3, g_mix.reshape(1, D), w, gains, bd, rope)


def _gelu_tanh(x):
    return 0.5 * x * (1.0 + jnp.tanh(math.sqrt(2.0 / math.pi) * (x + 0.044715 * (x * x * x))))


def _compress_kernel(r_ref, w1_ref, pe_ref, w2_ref, gk_ref, kc_ref, vc_ref):
    half = CMP_STRIDE * HEAD_DIM
    n_rows = r_ref.shape[2]
    for kv, out_ref in enumerate((kc_ref, vc_ref)):
        r = r_ref[0, kv]
        a = _dot(r, w1_ref[kv, 0:half, :])
        b = _dot(r, w1_ref[kv, half:2 * half, :])
        pb = _dot(pe_ref[kv], w1_ref[kv])
        hid = a + pltpu.roll(b, n_rows - 1, 0) + (pb[0:1] + pb[1:2])
        o = _dot(_gelu_tanh(hid).astype(BF16), w2_ref[kv])
        if kv == 0:
            o = o * lax.rsqrt(jnp.mean(o * o, axis=-1, keepdims=True) + EPS) * gk_ref[...]
        for g in range(NSA_KV_HEADS):
            out_ref[0, g] = o[g * LANE:(g + 1) * LANE].astype(BF16)


def _compress(kcv, pe_cmp, w_cmp1, w_cmp2, g_k0):
    B, S, _ = kcv.shape
    n_rows = S // CMP_STRIDE
    width = CMP_STRIDE * HEAD_DIM
    r = kcv.reshape(B, n_rows, CMP_STRIDE, 2, NSA_KV_HEADS, HEAD_DIM)
    r = r.transpose(0, 3, 4, 1, 2, 5).reshape(B, 2, NSA_KV_HEADS * n_rows, width)
    pe = pe_cmp.reshape(2, 1, CMP_LEN * HEAD_DIM)
    pe_hi = pe.astype(BF16)
    pe_lo = (pe - pe_hi.astype(F32)).astype(BF16)
    pe2 = jnp.concatenate([pe_hi, pe_lo, jnp.zeros((2, SUBLANE - 2, CMP_LEN * HEAD_DIM), BF16)], axis=1)
    shape = jax.ShapeDtypeStruct((B, NSA_KV_HEADS, n_rows, HEAD_DIM), BF16)
    spec = pl.BlockSpec((1, NSA_KV_HEADS, n_rows, HEAD_DIM), lambda b: (b, 0, 0, 0))
    return pl.pallas_call(
        _compress_kernel,
        grid=(B,),
        in_specs=[pl.BlockSpec((1, 2, NSA_KV_HEADS * n_rows, width), lambda b: (b, 0, 0, 0)),
                  pl.BlockSpec((2, 2 * width, CMP_HIDDEN), lambda b: (0, 0, 0)),
                  pl.BlockSpec((2, SUBLANE, 2 * width), lambda b: (0, 0, 0)),
                  pl.BlockSpec((2, CMP_HIDDEN, HEAD_DIM), lambda b: (0, 0, 0)),
                  pl.BlockSpec((1, HEAD_DIM), lambda b: (0, 0))],
        out_specs=[spec, spec],
        out_shape=[shape, shape],
        compiler_params=_params("arbitrary"),
    )(r, w_cmp1.astype(BF16), pe2, w_cmp2.astype(BF16), g_k0.reshape(1, HEAD_DIM))


def _softmax_cols(s):
    m = jnp.max(s, axis=0, keepdims=True)
    e = jnp.exp(s - m)
    return e / jnp.sum(e, axis=0, keepdims=True)


def _nsa_t_kernel(qnt_ref, qrt_ref, kc_ref, vct_ref, kv_ref, vt_ref, gt_ref, ovlt_ref, ot_ref,
                  selb_ref, qz_ref, s_ref, p_ref, m_ref, l_ref, alpha_ref, acc_ref, part_ref):
    tq, tk = TQ_ATT, TK_NSA
    qi = pl.program_id(1)
    seq = kv_ref.shape[1]
    n_cmp = (seq - CMP_LEN) // CMP_STRIDE + 1
    n_slc = seq // SLC_LEN
    t_row = qi * tq + lax.broadcasted_iota(I32, (1, tq), 1)
    c_col = lax.broadcasted_iota(I32, (kc_ref.shape[2], 1), 0)
    cmp_bias = jnp.where((c_col * CMP_STRIDE + (CMP_LEN - 1) <= t_row) & (c_col < n_cmp), 0.0, NEG)
    cmp_any = (t_row >= CMP_LEN - 1).astype(F32)
    b_col = lax.broadcasted_iota(I32, (n_slc, 1), 0)
    b_f = b_col.astype(F32)
    cur = t_row // SLC_LEN
    blk_valid = b_col <= cur
    forced = (b_col == 0) | (b_col == cur) | (b_col == cur - 1)
    w_start = pl.multiple_of(jnp.maximum(qi * tq - WIN, 0), tq)
    w_len = WIN + tq
    w_pos = w_start + lax.broadcasted_iota(I32, (w_len, 1), 0)
    w_bias = jnp.where((w_pos <= t_row) & (w_pos > t_row - WIN), 0.0, NEG)

    def tile4(a):
        return jnp.concatenate([a] * NSA_GROUP, axis=1)

    def gate(h, j):
        return gt_ref[0, 3 * h + j:3 * h + j + 1, :]

    def head_rows(h):
        return slice(h * HEAD_DIM, (h + 1) * HEAD_DIM)

    def q_lanes(n):
        return slice(n * tq, (n + 1) * tq)

    zeros = jnp.zeros((HEAD_DIM, NSA_GROUP * tq), BF16)
    for g in range(NSA_KV_HEADS):
        heads = [g * NSA_GROUP + n for n in range(NSA_GROUP)]
        qn4 = jnp.concatenate([qnt_ref[0, head_rows(h), :] for h in heads], axis=1)
        qr4 = jnp.concatenate([qrt_ref[0, head_rows(h), :] for h in heads], axis=1)
        p = _softmax_cols(_dot(kc_ref[0, g], qn4) + tile4(cmp_bias)) * tile4(cmp_any)
        o_cmp = _dot(vct_ref[0, g], p.astype(BF16))
        psum = p[:, q_lanes(0)]
        for n in range(1, NSA_GROUP):
            psum = psum + p[:, q_lanes(n)]
        imp = _dot(ovlt_ref[...], psum, precision=HIGHEST)
        score = jnp.where(blk_valid, imp + jnp.where(forced, FORCE_BONUS, 0.0), SENTINEL)
        sel = jnp.zeros((n_slc, tq), F32)
        for _ in range(SLC_TOPK):
            m = jnp.max(score, axis=0, keepdims=True)
            first = jnp.min(jnp.where(score == m, b_f, float(n_slc)), axis=0, keepdims=True)
            pick = (b_f == first) & (m > SENTINEL)
            sel = jnp.where(pick, 1.0, sel)
            score = jnp.where(pick, SENTINEL, score)
        sel_bias = (sel - 1.0) * (-NEG)
        for j in range(n_slc):
            selb_ref[g, j * SLC_LEN:(j + 1) * SLC_LEN, :] = jnp.broadcast_to(sel_bias[j:j + 1, :], (SLC_LEN, tq))
        qz = jnp.concatenate([qr4, zeros] if g == 0 else [zeros, qr4], axis=0)
        qz_ref[g] = qz
        kw = kv_ref[0, pl.ds(w_start, w_len), 2 * LANE:3 * LANE]
        pw = _softmax_cols(_dot(kw, qz) + tile4(w_bias))
        o_win = _dot(vt_ref[0, LANE + g * HEAD_DIM:LANE + (g + 1) * HEAD_DIM, pl.ds(w_start, w_len)],
                     pw.astype(BF16))
        for n, h in enumerate(heads):
            part_ref[head_rows(h), :] = gate(h, 0) * o_cmp[:, q_lanes(n)] + gate(h, 2) * o_win[:, q_lanes(n)]

    m_ref[...] = jnp.full(m_ref.shape, NEG, F32)
    l_ref[...] = jnp.zeros(l_ref.shape, F32)
    acc_ref[...] = jnp.zeros(acc_ref.shape, F32)

    def step(c, carry, causal=None):
        k0 = pl.multiple_of(c * tk, tk)
        k = kv_ref[0, pl.ds(k0, tk), 0:LANE]
        for g in range(NSA_KV_HEADS):
            bias = selb_ref[g, pl.ds(k0, tk), :]
            if causal is not None:
                bias = bias + causal
            s_ref[g] = _dot(k, qz_ref[g]) + tile4(bias)
        for g in range(NSA_KV_HEADS):
            s = s_ref[g]
            m = m_ref[g:g + 1, :]
            m_new = jnp.maximum(m, jnp.max(s, axis=0, keepdims=True))
            alpha = jnp.exp(m - m_new)
            p = jnp.exp(s - m_new)
            l_ref[g:g + 1, :] = alpha * l_ref[g:g + 1, :] + jnp.sum(p, axis=0, keepdims=True)
            p_ref[g] = p.astype(BF16)
            alpha_ref[g:g + 1, :] = alpha
            m_ref[g:g + 1, :] = m_new
        for g in range(NSA_KV_HEADS):
            vt = vt_ref[0, g * HEAD_DIM:(g + 1) * HEAD_DIM, pl.ds(k0, tk)]
            acc_ref[g] = alpha_ref[g:g + 1, :] * acc_ref[g] + _dot(vt, p_ref[g])
        return carry

    n_full = (qi * tq) // tk
    lax.fori_loop(0, n_full, step, 0)
    k_pos = n_full * tk + lax.broadcasted_iota(I32, (tk, 1), 0)
    step(n_full, 0, jnp.where(k_pos <= t_row, 0.0, NEG))

    for g in range(NSA_KV_HEADS):
        o_slc = acc_ref[g] / l_ref[g:g + 1, :]
        for n in range(NSA_GROUP):
            h = g * NSA_GROUP + n
            o = part_ref[head_rows(h), :] + gate(h, 1) * o_slc[:, q_lanes(n)]
            ot_ref[0, head_rows(h), :] = o.astype(BF16)


def _nsa_t(qn, qr, kcmp, vcmp, kv, gates):
    B, S, W = qn.shape
    tq, tk = TQ_ATT, TK_NSA
    n_cmp = (S - CMP_LEN) // CMP_STRIDE + 1
    n_slc = S // SLC_LEN
    n_rows = kcmp.shape[2]
    c0 = np.arange(n_rows)[None, :] * CMP_STRIDE
    s0 = np.arange(n_slc)[:, None] * SLC_LEN
    ovlt = np.clip(np.minimum(c0 + CMP_LEN, s0 + SLC_LEN) - np.maximum(c0, s0), 0, None) / CMP_LEN
    ovlt = ovlt * (np.arange(n_rows)[None, :] < n_cmp)
    qnt, qrt = qn.transpose(0, 2, 1), qr.transpose(0, 2, 1)
    vt = jnp.concatenate([kv[:, :, LANE:2 * LANE], kv[:, :, 3 * LANE:4 * LANE]], axis=-1).transpose(0, 2, 1)
    gt = gates.transpose(0, 2, 1)
    vct = vcmp.transpose(0, 1, 3, 2)
    wide = NSA_GROUP * tq
    ot = pl.pallas_call(
        _nsa_t_kernel,
        grid=(B, S // tq),
        in_specs=[pl.BlockSpec((1, W, tq), lambda b, q: (b, 0, q)),
                  pl.BlockSpec((1, W, tq), lambda b, q: (b, 0, q)),
                  pl.BlockSpec((1,) + kcmp.shape[1:], lambda b, q: (b, 0, 0, 0)),
                  pl.BlockSpec((1,) + vct.shape[1:], lambda b, q: (b, 0, 0, 0)),
                  pl.BlockSpec((1, S, kv.shape[2]), lambda b, q: (b, 0, 0)),
                  pl.BlockSpec((1, 2 * LANE, S), lambda b, q: (b, 0, 0)),
                  pl.BlockSpec((1, LANE, tq), lambda b, q: (b, 0, q)),
                  pl.BlockSpec((n_slc, n_rows), lambda b, q: (0, 0))],
        out_specs=pl.BlockSpec((1, W, tq), lambda b, q: (b, 0, q)),
        out_shape=jax.ShapeDtypeStruct((B, W, S), BF16),
        scratch_shapes=[pltpu.VMEM((NSA_KV_HEADS, S, tq), F32),
                        pltpu.VMEM((NSA_KV_HEADS, LANE, wide), BF16),
                        pltpu.VMEM((NSA_KV_HEADS, tk, wide), F32),
                        pltpu.VMEM((NSA_KV_HEADS, tk, wide), BF16),
                        pltpu.VMEM((NSA_KV_HEADS, wide), F32), pltpu.VMEM((NSA_KV_HEADS, wide), F32),
                        pltpu.VMEM((NSA_KV_HEADS, wide), F32),
                        pltpu.VMEM((NSA_KV_HEADS, HEAD_DIM, wide), F32),
                        pltpu.VMEM((W, tq), F32)],
        compiler_params=_params("arbitrary", "arbitrary"),
    )(qnt, qrt, kcmp, vct, kv, vt, gt, jnp.asarray(ovlt, F32))
    return ot.transpose(0, 2, 1)


def _diff_kernel(qt_ref, k_ref, vt_ref, lam_ref, go_ref, ot_ref,
                 qm_ref, m_ref, l_ref, alpha_ref, acc_ref, s_ref, p_ref):
    t = TQ_DIFF
    qi = pl.program_id(1)
    scale = DIFF_QK_DIM ** -0.5
    lq = lam_ref[...]
    lam = (jnp.exp(jnp.sum(lq[0:1] * lq[1:2], axis=-1, keepdims=True))
           - jnp.exp(jnp.sum(lq[2:3] * lq[3:4], axis=-1, keepdims=True)) + LAMBDA_INIT)
    diag_bias = jnp.where(lax.broadcasted_iota(I32, (t, 1), 0) <= lax.broadcasted_iota(I32, (1, t), 1),
                          0.0, NEG)
    row = lax.broadcasted_iota(I32, (LANE, 1), 0)

    def tile_of(h):
        return (h * 2 * DIFF_QK_DIM) // LANE * LANE

    for h in range(DIFF_HEADS):
        q128 = qt_ref[0, tile_of(h):tile_of(h) + LANE, :]
        for mp in range(2):
            off = h * 2 * DIFF_QK_DIM + mp * DIFF_QK_DIM - tile_of(h)
            qm_ref[2 * h + mp] = jnp.where((row >= off) & (row < off + DIFF_QK_DIM), q128, jnp.zeros_like(q128))
    m_ref[...] = jnp.full(m_ref.shape, NEG, F32)
    l_ref[...] = jnp.zeros(l_ref.shape, F32)
    acc_ref[...] = jnp.zeros(acc_ref.shape, F32)

    for h0 in range(0, DIFF_HEADS, DIFF_HEADS_PER_PASS):
        def step(c, carry, bias=None):
            k0 = pl.multiple_of(c * t, t)
            chains = [(h, j) for h in range(h0, h0 + DIFF_HEADS_PER_PASS) for j in (2 * h, 2 * h + 1)]
            for h, j in chains:
                k = k_ref[0, pl.ds(k0, t), tile_of(h):tile_of(h) + LANE]
                s = _dot(k, qm_ref[j])
                s_ref[j - 2 * h0] = s if bias is None else s + bias
            for h, j in chains:
                s = s_ref[j - 2 * h0]
                m = m_ref[j:j + 1, :]
                m_new = jnp.maximum(m, jnp.max(s, axis=0, keepdims=True))
                alpha = jnp.exp((m - m_new) * scale)
                p = jnp.exp(s * scale - m_new * scale)
                l_ref[j:j + 1, :] = alpha * l_ref[j:j + 1, :] + jnp.sum(p, axis=0, keepdims=True)
                p_ref[j - 2 * h0] = p.astype(BF16)
                alpha_ref[j:j + 1, :] = alpha
                m_ref[j:j + 1, :] = m_new
            for h, j in chains:
                vt = vt_ref[0, h * DIFF_V_DIM:(h + 1) * DIFF_V_DIM, pl.ds(k0, t)]
                acc_ref[j] = alpha_ref[j:j + 1, :] * acc_ref[j] + _dot(vt, p_ref[j - 2 * h0])
            return carry
        lax.fori_loop(0, qi, step, 0)
        step(qi, 0, diag_bias)

    for h in range(DIFF_HEADS):
        o = (acc_ref[2 * h] / l_ref[2 * h:2 * h + 1, :]
             - lam * (acc_ref[2 * h + 1] / l_ref[2 * h + 1:2 * h + 2, :]))
        o = o * lax.rsqrt(jnp.mean(o * o, axis=0, keepdims=True) + EPS) * go_ref[...] * (1.0 - LAMBDA_INIT)
        ot_ref[0, h * DIFF_V_DIM:(h + 1) * DIFF_V_DIM, :] = o.astype(BF16)


def _diff(dq, dkv, lam4, g_out):
    B, S, W = dq.shape
    t = TQ_DIFF
    qt = dq.transpose(0, 2, 1)
    vt = dkv[:, :, W:].transpose(0, 2, 1)
    ot = pl.pallas_call(
        _diff_kernel,
        grid=(B, S // t),
        in_specs=[pl.BlockSpec((1, W, t), lambda b, q: (b, 0, q)),
                  pl.BlockSpec((1, S, W), lambda b, q: (b, 0, 0)),
                  pl.BlockSpec((1, W, S), lambda b, q: (b, 0, 0)),
                  pl.BlockSpec(lam4.shape, lambda b, q: (0, 0)),
                  pl.BlockSpec((DIFF_V_DIM, 1), lambda b, q: (0, 0))],
        out_specs=pl.BlockSpec((1, W, t), lambda b, q: (b, 0, q)),
        out_shape=jax.ShapeDtypeStruct((B, W, S), BF16),
        scratch_shapes=[pltpu.VMEM((2 * DIFF_HEADS, LANE, t), BF16),
                        pltpu.VMEM((2 * DIFF_HEADS, t), F32), pltpu.VMEM((2 * DIFF_HEADS, t), F32),
                        pltpu.VMEM((2 * DIFF_HEADS, t), F32),
                        pltpu.VMEM((2 * DIFF_HEADS, DIFF_V_DIM, t), F32),
                        pltpu.VMEM((2 * DIFF_HEADS_PER_PASS, t, t), F32),
                        pltpu.VMEM((2 * DIFF_HEADS_PER_PASS, t, t), BF16)],
        compiler_params=_params("arbitrary", "arbitrary"),
    )(qt, dkv, vt, lam4, g_out.reshape(DIFF_V_DIM, 1))
    return ot.transpose(0, 2, 1)


def _outproj_kernel(x_ref, on_ref, od_ref, wo_ref, mod_ref, gffn_ref, wr_ref, br_ref,
                    x1_ref, h_ref, route_ref):
    half = on_ref.shape[2]
    gate1 = mod_ref[0, 2:3, :]
    shift2, scale2 = mod_ref[0, 3:4, :], mod_ref[0, 4:5, :]
    attn = _dot(on_ref[0], wo_ref[0:half, :]) + _dot(od_ref[0], wo_ref[half:2 * half, :])
    x1 = x_ref[0] + gate1 * attn
    x1_ref[0] = x1
    ms = jnp.mean(x1 * x1, axis=-1, keepdims=True)
    h = x1 * lax.rsqrt(ms + EPS) * gffn_ref[...] * (1.0 + scale2) + shift2
    h_ref[0] = h

    r = _dot(h, wr_ref[...], precision=HIGHEST) + br_ref[...]
    lg, le = r[:, :LANE], r[:, LANE:]
    lane = lax.broadcasted_iota(I32, (1, LANE), 1).astype(F32)
    lg = jnp.where(lane < N_EXPERT_GROUPS, lg, SENTINEL)
    mg = jnp.max(lg, axis=-1, keepdims=True)
    p_grp = 1.0 / jnp.sum(jnp.exp(lg - mg), axis=-1, keepdims=True)
    grp = jnp.min(jnp.where(lg == mg, lane, float(LANE)), axis=-1, keepdims=True)
    lo = grp * EXPERTS_PER_GROUP
    le = jnp.where((lane >= lo) & (lane < lo + EXPERTS_PER_GROUP), le, SENTINEL)
    v1 = jnp.max(le, axis=-1, keepdims=True)
    i1 = jnp.min(jnp.where(le == v1, lane, float(LANE)), axis=-1, keepdims=True)
    le = jnp.where(lane == i1, SENTINEL, le)
    v2 = jnp.max(le, axis=-1, keepdims=True)
    i2 = jnp.min(jnp.where(le == v2, lane, float(LANE)), axis=-1, keepdims=True)
    e2 = jnp.exp(v2 - v1)
    w1 = p_grp / (1.0 + e2)
    w2 = p_grp * e2 / (1.0 + e2)
    route_ref[0] = jnp.where(lane == 0, i1, jnp.where(lane == 1, i2,
                             jnp.where(lane == 2, w1, jnp.where(lane == 3, w2, 0.0))))


def _outproj(x, o_nsa, o_diff, w_out, mod3, g_ffn, w_rg, b_rg, w_re, b_re):
    B, S, D = x.shape
    tm = TM_PROJ
    half = o_nsa.shape[2]
    wr = jnp.zeros((D, 2 * LANE), F32).at[:, :N_EXPERT_GROUPS].set(w_rg)
    wr = wr.at[:, LANE:LANE + N_EXPERTS].set(w_re)
    br = jnp.zeros((1, 2 * LANE), F32).at[0, :N_EXPERT_GROUPS].set(b_rg)
    br = br.at[0, LANE:LANE + N_EXPERTS].set(b_re)

    def tok(width):
        return pl.BlockSpec((1, tm, width), lambda b, s: (b, s, 0))

    return pl.pallas_call(
        _outproj_kernel,
        grid=(B, S // tm),
        in_specs=[tok(D), tok(half), tok(half),
                  pl.BlockSpec((D, D), lambda b, s: (0, 0)),
                  pl.BlockSpec((1, 6, D), lambda b, s: (b, 0, 0)),
                  pl.BlockSpec((1, D), lambda b, s: (0, 0)),
                  pl.BlockSpec((D, 2 * LANE), lambda b, s: (0, 0)),
                  pl.BlockSpec((1, 2 * LANE), lambda b, s: (0, 0))],
        out_specs=[tok(D), tok(D), tok(LANE)],
        out_shape=[jax.ShapeDtypeStruct((B, S, D), F32), jax.ShapeDtypeStruct((B, S, D), F32),
                   jax.ShapeDtypeStruct((B, S, LANE), F32)],
        compiler_params=_params("arbitrary", "arbitrary"),
    )(x, o_nsa, o_diff, w_out.astype(BF16), mod3, g_ffn.reshape(1, D), wr, br)


def _row_copy(src_hbm, dst, row, slot, sem):
    return pltpu.make_async_copy(src_hbm.at[pl.ds(row, 1)], dst.at[pl.ds(slot, 1)], sem)


def _prefetch_rows(i, n_steps, n_rows, idx_hbm, idx_smem, idx_sem, src_hbm, rows, row_sem):
    slot = i % 2

    def idx_copy(step, s):
        return pltpu.make_async_copy(idx_hbm.at[step], idx_smem.at[s], idx_sem.at[s])

    def issue_rows(s):
        def body(r, carry):
            _row_copy(src_hbm, rows.at[s], idx_smem[s, r // LANE, r % LANE], r, row_sem.at[s]).start()
            return carry
        lax.fori_loop(0, n_rows, body, 0, unroll=8)

    @pl.when(i == 0)
    def _():
        idx_copy(0, 0).start()
        idx_copy(0, 0).wait()
        issue_rows(0)

        @pl.when(n_steps > 1)
        def _():
            idx_copy(1, 1).start()

    @pl.when(i + 1 < n_steps)
    def _():
        idx_copy(i + 1, 1 - slot).wait()
        issue_rows(1 - slot)

        @pl.when(i + 2 < n_steps)
        def _():
            idx_copy(i + 2, slot).start()


def _wait_rows(n_rows, src_hbm, dst, sem):
    pltpu.make_async_copy(src_hbm.at[pl.ds(0, n_rows)], dst, sem).wait()


def _expert_kernel(blk_e_ref, n_used_ref, idx_hbm, h_hbm, wg_ref, wu_ref, wd_ref, o_ref,
                   idx_smem, xbuf, idx_sem, row_sem):
    i = pl.program_id(0)
    n_used = n_used_ref[0]
    slot = i % 2
    _prefetch_rows(i, n_used, MOE_BLOCK, idx_hbm, idx_smem, idx_sem, h_hbm, xbuf, row_sem)

    @pl.when(i < n_used)
    def _():
        _wait_rows(MOE_BLOCK, h_hbm, xbuf.at[slot], row_sem.at[slot])
        xb = xbuf[slot].astype(BF16)
        a = _dot(xb, wg_ref[0])
        u = _dot(xb, wu_ref[0])
        act = a * (1.0 / (1.0 + jnp.exp(-a))) * u
        o_ref[...] = _dot(act.astype(BF16), wd_ref[0])

    @pl.when(i >= n_used)
    def _():
        o_ref[...] = jnp.zeros_like(o_ref)


def _experts(blk_e, n_used, row_src, h, w_g, w_u, w_d):
    bm = MOE_BLOCK
    n_blk = row_src.shape[0]
    D = h.shape[1]
    H = w_g.shape[2]
    grid_spec = pltpu.PrefetchScalarGridSpec(
        num_scalar_prefetch=2,
        grid=(n_blk,),
        in_specs=[pl.BlockSpec(memory_space=pl.ANY), pl.BlockSpec(memory_space=pl.ANY),
                  pl.BlockSpec((1, D, H), lambda i, be, nu: (be[i], 0, 0)),
                  pl.BlockSpec((1, D, H), lambda i, be, nu: (be[i], 0, 0)),
                  pl.BlockSpec((1, H, D), lambda i, be, nu: (be[i], 0, 0))],
        out_specs=pl.BlockSpec((bm, D), lambda i, be, nu: (i, 0)),
        scratch_shapes=[pltpu.SMEM((2, SUBLANE, LANE), I32), pltpu.VMEM((2, bm, D), h.dtype),
                        pltpu.SemaphoreType.DMA((2,)), pltpu.SemaphoreType.DMA((2,))])
    return pl.pallas_call(
        _expert_kernel,
        grid_spec=grid_spec,
        out_shape=jax.ShapeDtypeStruct((n_blk * bm, D), F32),
        compiler_params=_params("arbitrary"),
    )(blk_e, n_used, row_src, h, w_g.astype(BF16), w_u.astype(BF16), w_d.astype(BF16))


def _combine_kernel(pos_hbm, yb_hbm, x1_ref, route_ref, mod_ref, o_ref, idx_smem, rows_ref, idx_sem, row_sem):
    i = pl.program_id(0)
    slot = i % 2
    _prefetch_rows(i, pl.num_programs(0), DMA_ROWS, pos_hbm, idx_smem, idx_sem, yb_hbm, rows_ref, row_sem)
    _wait_rows(DMA_ROWS, yb_hbm, rows_ref.at[slot], row_sem.at[slot])
    tm = DMA_ROWS // 2
    gate2 = mod_ref[0, 5:6, :]
    y = rows_ref[slot, 0:tm, :] * route_ref[:, 2:3] + rows_ref[slot, tm:2 * tm, :] * route_ref[:, 3:4]
    o_ref[...] = x1_ref[...] + gate2 * y


def _combine(pos, yb, x1, route, mod3, seq):
    T, D = x1.shape
    tm = DMA_ROWS // 2
    per_seq = seq // tm
    return pl.pallas_call(
        _combine_kernel,
        grid=(T // tm,),
        in_specs=[pl.BlockSpec(memory_space=pl.ANY), pl.BlockSpec(memory_space=pl.ANY),
                  pl.BlockSpec((tm, D), lambda i: (i, 0)),
                  pl.BlockSpec((tm, LANE), lambda i: (i, 0)),
                  pl.BlockSpec((1, 6, D), lambda i: (i // per_seq, 0, 0))],
        out_specs=pl.BlockSpec((tm, D), lambda i: (i, 0)),
        out_shape=jax.ShapeDtypeStruct((T, D), F32),
        scratch_shapes=[pltpu.SMEM((2, SUBLANE, LANE), I32), pltpu.VMEM((2, DMA_ROWS, D), F32),
                        pltpu.SemaphoreType.DMA((2,)), pltpu.SemaphoreType.DMA((2,))],
        compiler_params=_params("arbitrary"),
    )(pos, yb, x1, route, mod3)


def _dispatch_plan(route, T):
    bm = MOE_BLOCK
    e_flat = route[:, :2].astype(I32).reshape(-1)
    onehot = (e_flat[:, None] == jnp.arange(N_EXPERTS, dtype=I32)[None, :]).astype(I32)
    csum = jnp.cumsum(onehot, axis=0)
    rank = jnp.take_along_axis(csum, e_flat[:, None], axis=1)[:, 0] - 1
    counts = csum[-1]
    padded = (counts + bm - 1) // bm * bm
    pend = jnp.cumsum(padded)
    dest = (pend - padded)[e_flat] + rank
    n_blk = (2 * T + bm - 1) // bm + N_EXPERTS
    n_blk = (n_blk * bm + DMA_ROWS - 1) // DMA_ROWS * DMA_ROWS // bm
    row_src = jnp.zeros((n_blk * bm,), I32).at[dest].set(jnp.arange(2 * T, dtype=I32) // 2)
    blk_start = jnp.arange(n_blk, dtype=I32) * bm
    blk_e = jnp.minimum(jnp.sum((pend[None, :] <= blk_start[:, None]).astype(I32), axis=1), N_EXPERTS - 1)
    n_used = (pend[-1:] // bm).astype(I32)
    return dest, row_src, blk_e, n_used


def kernel(x, c, w_ada, b_ada, g_norm_mix, g_norm_ffn, w_in, g_nsa_q, g_nsa_k, pe_cmp, w_cmp1, w_cmp2, g_diff_q, g_diff_k, lam_q1, lam_k1, lam_q2, lam_k2, g_diff_out, w_out, w_router_group, b_router_group, w_router_expert, b_router_expert, w_exp_gate, w_exp_up, w_exp_down):
    B, S, D = x.shape
    T = B * S
    assert w_ada.shape[0] == 1, "single-layer operation"
    assert S % TK_NSA == 0 and S % TQ_DIFF == 0 and S % TM_PROJ == 0 and S % (DMA_ROWS // 2) == 0
    assert TK_NSA % TQ_ATT == 0 and S // CMP_STRIDE == LANE

    mod3 = _ada(c, w_ada[0], b_ada[0]).reshape(B, 6, D)
    qn, qr, kcv, kv, gates, dq, dkv = _inproj(
        x, mod3, g_norm_mix[0], w_in[0], g_nsa_q[0], g_nsa_k[0], g_diff_q[0], g_diff_k[0])
    kcmp, vcmp = _compress(kcv, pe_cmp[0], w_cmp1[0], w_cmp2[0], g_nsa_k[0, 0])
    o_nsa = _nsa_t(qn, qr, kcmp, vcmp, kv, gates)
    lam4 = jnp.stack([lam_q1[0], lam_k1[0], lam_q2[0], lam_k2[0]])
    o_diff = _diff(dq, dkv, lam4, g_diff_out[0])
    x1, h2, route = _outproj(x, o_nsa, o_diff, w_out[0], mod3, g_norm_ffn[0],
                             w_router_group[0], b_router_group[0], w_router_expert[0], b_router_expert[0])

    route = route.reshape(T, LANE)
    dest, row_src, blk_e, n_used = _dispatch_plan(route, T)
    idx_rows = MOE_BLOCK // LANE
    row_src = jnp.pad(row_src.reshape(-1, idx_rows, LANE), ((0, 0), (0, SUBLANE - idx_rows), (0, 0)))
    yb = _experts(blk_e, n_used, row_src, h2.reshape(T, D), w_exp_gate[0], w_exp_up[0], w_exp_down[0])
    tm = DMA_ROWS // 2
    pos = dest.reshape(T // tm, tm, 2).transpose(0, 2, 1).reshape(T // tm, SUBLANE, LANE)
    out = _combine(pos, yb, x1.reshape(T, D), route, mod3, S)
    return out.reshape(B, S, D)
```

```python
import math

import numpy as np
import jax
import jax.numpy as jnp
from jax import lax
from jax.experimental import pallas as pl
from jax.experimental.pallas import tpu as pltpu

F32, BF16, I32 = jnp.float32, jnp.bfloat16, jnp.int32
HIGHEST = lax.Precision.HIGHEST

HEAD_DIM = 64
NSA_HEADS = 8
NSA_KV_HEADS = 2
NSA_GROUP = NSA_HEADS // NSA_KV_HEADS
CMP_LEN = 32
CMP_STRIDE = 16
CMP_HIDDEN = 256
SLC_LEN = 64
SLC_TOPK = 6
WIN = 256
DIFF_QK_DIM = 32
DIFF_V_DIM = 64
DIFF_HEADS = 8
ROPE_THETA = 500000.0
N_EXPERT_GROUPS = 4
EXPERTS_PER_GROUP = 8
N_EXPERTS = N_EXPERT_GROUPS * EXPERTS_PER_GROUP
EPS = 1e-6
NEG = -1e30
FORCE_BONUS = 1e4
LAMBDA_INIT = 0.8 - 0.6 * math.exp(-0.3 * 0)

LANE = 128
SUBLANE = 8
VMEM_LIMIT = 48 * 1024 * 1024

TM_PROJ = 512
TQ_ATT = 256
TK_NSA = 256
TQ_DIFF = 256
DIFF_HEADS_PER_PASS = 8
MOE_BLOCK = 512
DMA_ROWS = SUBLANE * LANE
SENTINEL = -3e38


def _dot(a, b, **kw):
    return jnp.dot(a, b, preferred_element_type=F32, **kw)


def _params(*sem):
    return pltpu.CompilerParams(dimension_semantics=sem, vmem_limit_bytes=VMEM_LIMIT)


def _ada_kernel(c_ref, w_ref, b_ref, o_ref):
    o_ref[...] = _dot(c_ref[...], w_ref[...], precision=HIGHEST) + b_ref[...]


def _ada(c, w, b):
    B, D = c.shape
    n = w.shape[1]
    return pl.pallas_call(
        _ada_kernel,
        grid=(n // D,),
        in_specs=[pl.BlockSpec((B, D), lambda j: (0, 0)),
                  pl.BlockSpec((D, D), lambda j: (0, j)),
                  pl.BlockSpec((1, D), lambda j: (0, j))],
        out_specs=pl.BlockSpec((B, D), lambda j: (0, j)),
        out_shape=jax.ShapeDtypeStruct((B, n), F32),
        compiler_params=_params("arbitrary"),
    )(c, w, b.reshape(1, n))


_C_Q, _C_KCV, _C_KS, _C_KW, _C_DQ, _C_DK, _C_DV, _C_GATE, _C_END = (
    0, 512, 768, 1024, 1280, 1792, 2304, 2816, 2944)


def _seg_rms(y, bd, inv_n):
    y2 = y * y
    hi = y2.astype(BF16)
    lo = (y2 - hi.astype(F32)).astype(BF16)
    ss = _dot(hi, bd) + _dot(lo, bd)
    return y * lax.rsqrt(ss * inv_n + EPS)


def _rope(y, c, sa, sb, half):
    return y * c + pltpu.roll(y, LANE - half, 1) * sa + pltpu.roll(y, half, 1) * sb


def _inproj_kernel(x_ref, mod_ref, gmix_ref, w_ref, gains_ref, bd_ref, rope_ref,
                   qn_ref, qr_ref, kcv_ref, kv_ref, gate_ref, dq_ref, dkv_ref):
    x = x_ref[0]
    shift, scale = mod_ref[0, 0:1, :], mod_ref[0, 1:2, :]
    ms = jnp.mean(x * x, axis=-1, keepdims=True)
    h = x * lax.rsqrt(ms + EPS) * gmix_ref[...] * (1.0 + scale) + shift
    hb = h.astype(BF16)
    bd64, bd32 = bd_ref[0], bd_ref[1]
    cn, san, sbn = rope_ref[0], rope_ref[1], rope_ref[2]
    cd, sad, sbd = rope_ref[3], rope_ref[4], rope_ref[5]
    half_n, half_d = HEAD_DIM // 8, DIFF_QK_DIM // 8

    def mm(c0):
        y = _dot(hb, w_ref[:, c0:c0 + 2 * LANE])
        return y[:, :LANE], y[:, LANE:]

    for c in range(2):
        for j, y in enumerate(mm(_C_Q + c * 2 * LANE)):
            col = (2 * c + j) * LANE
            yn = _seg_rms(y, bd64, 1.0 / HEAD_DIM) * gains_ref[0:1, :]
            qn_ref[0, :, col:col + LANE] = yn.astype(BF16)
            qr_ref[0, :, col:col + LANE] = _rope(yn, cn, san, sbn, half_n).astype(BF16)

    kc, vc = mm(_C_KCV)
    kcv_ref[0, :, 0:LANE] = kc.astype(BF16)
    kcv_ref[0, :, LANE:2 * LANE] = vc.astype(BF16)
    for i, c0 in enumerate((_C_KS, _C_KW)):
        k, v = mm(c0)
        kn = _seg_rms(k, bd64, 1.0 / HEAD_DIM) * gains_ref[1 + i:2 + i, :]
        kv_ref[0, :, 2 * i * LANE:(2 * i + 1) * LANE] = _rope(kn, cn, san, sbn, half_n).astype(BF16)
        kv_ref[0, :, (2 * i + 1) * LANE:(2 * i + 2) * LANE] = v.astype(BF16)

    for i, (c0, out_ref) in enumerate(((_C_DQ, dq_ref), (_C_DK, dkv_ref))):
        for c in range(2):
            for j, y in enumerate(mm(c0 + c * 2 * LANE)):
                col = (2 * c + j) * LANE
                yn = _seg_rms(y, bd32, 1.0 / DIFF_QK_DIM) * gains_ref[3 + i:4 + i, :]
                out_ref[0, :, col:col + LANE] = _rope(yn, cd, sad, sbd, half_d).astype(BF16)
    for c in range(2):
        for j, y in enumerate(mm(_C_DV + c * 2 * LANE)):
            col = 4 * LANE + (2 * c + j) * LANE
            dkv_ref[0, :, col:col + LANE] = y.astype(BF16)

    g = _dot(hb, w_ref[:, _C_GATE:_C_END])
    gate_ref[0] = 1.0 / (1.0 + jnp.exp(-g))


def _rope_tables(seq, head_dim):
    rot = head_dim // 4
    half = rot // 2
    inv = ROPE_THETA ** (-jnp.arange(0, rot, 2, dtype=F32) / rot)
    ang = jnp.arange(seq, dtype=F32)[:, None] * inv[None, :]
    cos, sin = jnp.cos(ang), jnp.sin(ang)
    pad = jnp.zeros((seq, head_dim - 2 * half), F32)
    c = jnp.concatenate([cos, cos, pad + 1.0], axis=-1)
    sa = jnp.concatenate([-sin, jnp.zeros_like(sin), pad], axis=-1)
    sb = jnp.concatenate([jnp.zeros_like(sin), sin, pad], axis=-1)
    reps = LANE // head_dim
    return jnp.stack([jnp.tile(t, (1, reps)) for t in (c, sa, sb)])


def _block_diag_ones(seg):
    i = np.arange(LANE)
    return jnp.asarray((i[:, None] // seg) == (i[None, :] // seg), dtype=BF16)


def _inproj(x, mod3, g_mix, w_in, g_nsa_q, g_nsa_k, g_diff_q, g_diff_k):
    B, S, D = x.shape
    tm = TM_PROJ
    gate_w = w_in[:, 1280:1304]
    w = jnp.concatenate(
        [w_in[:, :1280], w_in[:, 1304:], jnp.pad(gate_w, ((0, 0), (0, LANE - gate_w.shape[1])))],
        axis=1).astype(BF16)
    scale = HEAD_DIM ** -0.5
    gains = jnp.stack([
        jnp.tile(g_nsa_q, LANE // HEAD_DIM) * scale,
        jnp.tile(g_nsa_k[1], LANE // HEAD_DIM), jnp.tile(g_nsa_k[2], LANE // HEAD_DIM),
        jnp.tile(g_diff_q, LANE // DIFF_QK_DIM), jnp.tile(g_diff_k, LANE // DIFF_QK_DIM),
        jnp.zeros(LANE), jnp.zeros(LANE), jnp.zeros(LANE)]).astype(F32)
    bd = jnp.stack([_block_diag_ones(HEAD_DIM), _block_diag_ones(DIFF_QK_DIM)])
    rope = jnp.concatenate([_rope_tables(S, HEAD_DIM), _rope_tables(S, DIFF_QK_DIM)])

    def tok(width):
        return pl.BlockSpec((1, tm, width), lambda s, b: (b, s, 0))

    def out(width, dt=BF16):
        return jax.ShapeDtypeStruct((B, S, width), dt)

    return pl.pallas_call(
        _inproj_kernel,
        grid=(S // tm, B),
        in_specs=[tok(D),
                  pl.BlockSpec((1, 6, D), lambda s, b: (b, 0, 0)),
                  pl.BlockSpec((1, D), lambda s, b: (0, 0)),
                  pl.BlockSpec((D, _C_END), lambda s, b: (0, 0)),
                  pl.BlockSpec((SUBLANE, LANE), lambda s, b: (0, 0)),
                  pl.BlockSpec((2, LANE, LANE), lambda s, b: (0, 0, 0)),
                  pl.BlockSpec((6, tm, LANE), lambda s, b: (0, s, 0))],
        out_specs=[tok(512), tok(512), tok(256), tok(512), tok(LANE), tok(512), tok(1024)],
        out_shape=[out(512), out(512), out(256), out(512), out(LANE, F32), out(512), out(1024)],
        compiler_params=_params("arbitrary", "arbitrary"),
    )(x, mod3, g_mix.reshape(1, D), w, gains, bd, rope)


def _gelu_tanh(x):
    return 0.5 * x * (1.0 + jnp.tanh(math.sqrt(2.0 / math.pi) * (x + 0.044715 * (x * x * x))))


def _compress_kernel(r_ref, w1_ref, pe_ref, w2_ref, gk_ref, kc_ref, vc_ref):
    half = CMP_STRIDE * HEAD_DIM
    n_rows = r_ref.shape[2]
    for kv, out_ref in enumerate((kc_ref, vc_ref)):
        r = r_ref[0, kv]
        a = _dot(r, w1_ref[kv, 0:half, :])
        b = _dot(r, w1_ref[kv, half:2 * half, :])
        pb = _dot(pe_ref[kv], w1_ref[kv])
        hid = a + pltpu.roll(b, n_rows - 1, 0) + (pb[0:1] + pb[1:2])
        o = _dot(_gelu_tanh(hid).astype(BF16), w2_ref[kv])
        if kv == 0:
            o = o * lax.rsqrt(jnp.mean(o * o, axis=-1, keepdims=True) + EPS) * gk_ref[...]
        for g in range(NSA_KV_HEADS):
            out_ref[0, g] = o[g * LANE:(g + 1) * LANE].astype(BF16)


def _compress(kcv, pe_cmp, w_cmp1, w_cmp2, g_k0):
    B, S, _ = kcv.shape
    n_rows = S // CMP_STRIDE
    width = CMP_STRIDE * HEAD_DIM
    r = kcv.reshape(B, n_rows, CMP_STRIDE, 2, NSA_KV_HEADS, HEAD_DIM)
    r = r.transpose(0, 3, 4, 1, 2, 5).reshape(B, 2, NSA_KV_HEADS * n_rows, width)
    pe = pe_cmp.reshape(2, 1, CMP_LEN * HEAD_DIM)
    pe_hi = pe.astype(BF16)
    pe_lo = (pe - pe_hi.astype(F32)).astype(BF16)
    pe2 = jnp.concatenate([pe_hi, pe_lo, jnp.zeros((2, SUBLANE - 2, CMP_LEN * HEAD_DIM), BF16)], axis=1)
    shape = jax.ShapeDtypeStruct((B, NSA_KV_HEADS, n_rows, HEAD_DIM), BF16)
    spec = pl.BlockSpec((1, NSA_KV_HEADS, n_rows, HEAD_DIM), lambda b: (b, 0, 0, 0))
    return pl.pallas_call(
        _compress_kernel,
        grid=(B,),
        in_specs=[pl.BlockSpec((1, 2, NSA_KV_HEADS * n_rows, width), lambda b: (b, 0, 0, 0)),
                  pl.BlockSpec((2, 2 * width, CMP_HIDDEN), lambda b: (0, 0, 0)),
                  pl.BlockSpec((2, SUBLANE, 2 * width), lambda b: (0, 0, 0)),
                  pl.BlockSpec((2, CMP_HIDDEN, HEAD_DIM), lambda b: (0, 0, 0)),
                  pl.BlockSpec((1, HEAD_DIM), lambda b: (0, 0))],
        out_specs=[spec, spec],
        out_shape=[shape, shape],
        compiler_params=_params("arbitrary"),
    )(r, w_cmp1.astype(BF16), pe2, w_cmp2.astype(BF16), g_k0.reshape(1, HEAD_DIM))


def _softmax_cols(s):
    m = jnp.max(s, axis=0, keepdims=True)
    e = jnp.exp(s - m)
    return e / jnp.sum(e, axis=0, keepdims=True)


def _nsa_t_kernel(qnt_ref, qrt_ref, kc_ref, vct_ref, kv_ref, vt_ref, gt_ref, ovlt_ref, ot_ref,
                  selb_ref, qz_ref, s_ref, p_ref, m_ref, l_ref, alpha_ref, acc_ref, part_ref):
    tq, tk = TQ_ATT, TK_NSA
    qi = pl.program_id(1)
    seq = kv_ref.shape[1]
    n_cmp = (seq - CMP_LEN) // CMP_STRIDE + 1
    n_slc = seq // SLC_LEN
    t_row = qi * tq + lax.broadcasted_iota(I32, (1, tq), 1)
    c_col = lax.broadcasted_iota(I32, (kc_ref.shape[2], 1), 0)
    cmp_bias = jnp.where((c_col * CMP_STRIDE + (CMP_LEN - 1) <= t_row) & (c_col < n_cmp), 0.0, NEG)
    cmp_any = (t_row >= CMP_LEN - 1).astype(F32)
    b_col = lax.broadcasted_iota(I32, (n_slc, 1), 0)
    b_f = b_col.astype(F32)
    cur = t_row // SLC_LEN
    blk_valid = b_col <= cur
    forced = (b_col == 0) | (b_col == cur) | (b_col == cur - 1)
    w_start = pl.multiple_of(jnp.maximum(qi * tq - WIN, 0), tq)
    w_len = WIN + tq
    w_pos = w_start + lax.broadcasted_iota(I32, (w_len, 1), 0)
    w_bias = jnp.where((w_pos <= t_row) & (w_pos > t_row - WIN), 0.0, NEG)

    def tile4(a):
        return jnp.concatenate([a] * NSA_GROUP, axis=1)

    def gate(h, j):
        return gt_ref[0, 3 * h + j:3 * h + j + 1, :]

    def head_rows(h):
        return slice(h * HEAD_DIM, (h + 1) * HEAD_DIM)

    def q_lanes(n):
        return slice(n * tq, (n + 1) * tq)

    zeros = jnp.zeros((HEAD_DIM, NSA_GROUP * tq), BF16)
    for g in range(NSA_KV_HEADS):
        heads = [g * NSA_GROUP + n for n in range(NSA_GROUP)]
        qn4 = jnp.concatenate([qnt_ref[0, head_rows(h), :] for h in heads], axis=1)
        qr4 = jnp.concatenate([qrt_ref[0, head_rows(h), :] for h in heads], axis=1)
        p = _softmax_cols(_dot(kc_ref[0, g], qn4) + tile4(cmp_bias)) * tile4(cmp_any)
        o_cmp = _dot(vct_ref[0, g], p.astype(BF16))
        psum = p[:, q_lanes(0)]
        for n in range(1, NSA_GROUP):
            psum = psum + p[:, q_lanes(n)]
        imp = _dot(ovlt_ref[...], psum, precision=HIGHEST)
        score = jnp.where(blk_valid, imp + jnp.where(forced, FORCE_BONUS, 0.0), SENTINEL)
        sel = jnp.zeros((n_slc, tq), F32)
        for _ in range(SLC_TOPK):
            m = jnp.max(score, axis=0, keepdims=True)
            first = jnp.min(jnp.where(score == m, b_f, float(n_slc)), axis=0, keepdims=True)
            pick = (b_f == first) & (m > SENTINEL)
            sel = jnp.where(pick, 1.0, sel)
            score = jnp.where(pick, SENTINEL, score)
        sel_bias = (sel - 1.0) * (-NEG)
        for j in range(n_slc):
            selb_ref[g, j * SLC_LEN:(j + 1) * SLC_LEN, :] = jnp.broadcast_to(sel_bias[j:j + 1, :], (SLC_LEN, tq))
        qz = jnp.concatenate([qr4, zeros] if g == 0 else [zeros, qr4], axis=0)
        qz_ref[g] = qz
        kw = kv_ref[0, pl.ds(w_start, w_len), 2 * LANE:3 * LANE]
        pw = _softmax_cols(_dot(kw, qz) + tile4(w_bias))
        o_win = _dot(vt_ref[0, LANE + g * HEAD_DIM:LANE + (g + 1) * HEAD_DIM, pl.ds(w_start, w_len)],
                     pw.astype(BF16))
        for n, h in enumerate(heads):
            part_ref[head_rows(h), :] = gate(h, 0) * o_cmp[:, q_lanes(n)] + gate(h, 2) * o_win[:, q_lanes(n)]

    m_ref[...] = jnp.full(m_ref.shape, NEG, F32)
    l_ref[...] = jnp.zeros(l_ref.shape, F32)
    acc_ref[...] = jnp.zeros(acc_ref.shape, F32)

    def step(c, carry, causal=None):
        k0 = pl.multiple_of(c * tk, tk)
        k = kv_ref[0, pl.ds(k0, tk), 0:LANE]
        for g in range(NSA_KV_HEADS):
            bias = selb_ref[g, pl.ds(k0, tk), :]
            if causal is not None:
                bias = bias + causal
            s_ref[g] = _dot(k, qz_ref[g]) + tile4(bias)
        for g in range(NSA_KV_HEADS):
            s = s_ref[g]
            m = m_ref[g:g + 1, :]
            m_new = jnp.maximum(m, jnp.max(s, axis=0, keepdims=True))
            alpha = jnp.exp(m - m_new)
            p = jnp.exp(s - m_new)
            l_ref[g:g + 1, :] = alpha * l_ref[g:g + 1, :] + jnp.sum(p, axis=0, keepdims=True)
            p_ref[g] = p.astype(BF16)
            alpha_ref[g:g + 1, :] = alpha
            m_ref[g:g + 1, :] = m_new
        for g in range(NSA_KV_HEADS):
            vt = vt_ref[0, g * HEAD_DIM:(g + 1) * HEAD_DIM, pl.ds(k0, tk)]
            acc_ref[g] = alpha_ref[g:g + 1, :] * acc_ref[g] + _dot(vt, p_ref[g])
        return carry

    n_full = (qi * tq) // tk
    lax.fori_loop(0, n_full, step, 0)
    k_pos = n_full * tk + lax.broadcasted_iota(I32, (tk, 1), 0)
    step(n_full, 0, jnp.where(k_pos <= t_row, 0.0, NEG))

    for g in range(NSA_KV_HEADS):
        o_slc = acc_ref[g] / l_ref[g:g + 1, :]
        for n in range(NSA_GROUP):
            h = g * NSA_GROUP + n
            o = part_ref[head_rows(h), :] + gate(h, 1) * o_slc[:, q_lanes(n)]
            ot_ref[0, head_rows(h), :] = o.astype(BF16)


def _nsa_t(qn, qr, kcmp, vcmp, kv, gates):
    B, S, W = qn.shape
    tq, tk = TQ_ATT, TK_NSA
    n_cmp = (S - CMP_LEN) // CMP_STRIDE + 1
    n_slc = S // SLC_LEN
    n_rows = kcmp.shape[2]
    c0 = np.arange(n_rows)[None, :] * CMP_STRIDE
    s0 = np.arange(n_slc)[:, None] * SLC_LEN
    ovlt = np.clip(np.minimum(c0 + CMP_LEN, s0 + SLC_LEN) - np.maximum(c0, s0), 0, None) / CMP_LEN
    ovlt = ovlt * (np.arange(n_rows)[None, :] < n_cmp)
    qnt, qrt = qn.transpose(0, 2, 1), qr.transpose(0, 2, 1)
    vt = jnp.concatenate([kv[:, :, LANE:2 * LANE], kv[:, :, 3 * LANE:4 * LANE]], axis=-1).transpose(0, 2, 1)
    gt = gates.transpose(0, 2, 1)
    vct = vcmp.transpose(0, 1, 3, 2)
    wide = NSA_GROUP * tq
    ot = pl.pallas_call(
        _nsa_t_kernel,
        grid=(B, S // tq),
        in_specs=[pl.BlockSpec((1, W, tq), lambda b, q: (b, 0, q)),
                  pl.BlockSpec((1, W, tq), lambda b, q: (b, 0, q)),
                  pl.BlockSpec((1,) + kcmp.shape[1:], lambda b, q: (b, 0, 0, 0)),
                  pl.BlockSpec((1,) + vct.shape[1:], lambda b, q: (b, 0, 0, 0)),
                  pl.BlockSpec((1, S, kv.shape[2]), lambda b, q: (b, 0, 0)),
                  pl.BlockSpec((1, 2 * LANE, S), lambda b, q: (b, 0, 0)),
                  pl.BlockSpec((1, LANE, tq), lambda b, q: (b, 0, q)),
                  pl.BlockSpec((n_slc, n_rows), lambda b, q: (0, 0))],
        out_specs=pl.BlockSpec((1, W, tq), lambda b, q: (b, 0, q)),
        out_shape=jax.ShapeDtypeStruct((B, W, S), BF16),
        scratch_shapes=[pltpu.VMEM((NSA_KV_HEADS, S, tq), F32),
                        pltpu.VMEM((NSA_KV_HEADS, LANE, wide), BF16),
                        pltpu.VMEM((NSA_KV_HEADS, tk, wide), F32),
                        pltpu.VMEM((NSA_KV_HEADS, tk, wide), BF16),
                        pltpu.VMEM((NSA_KV_HEADS, wide), F32), pltpu.VMEM((NSA_KV_HEADS, wide), F32),
                        pltpu.VMEM((NSA_KV_HEADS, wide), F32),
                        pltpu.VMEM((NSA_KV_HEADS, HEAD_DIM, wide), F32),
                        pltpu.VMEM((W, tq), F32)],
        compiler_params=_params("arbitrary", "arbitrary"),
    )(qnt, qrt, kcmp, vct, kv, vt, gt, jnp.asarray(ovlt, F32))
    return ot.transpose(0, 2, 1)


def _diff_kernel(qt_ref, k_ref, vt_ref, lam_ref, go_ref, ot_ref,
                 qm_ref, m_ref, l_ref, alpha_ref, acc_ref, s_ref, p_ref):
    t = TQ_DIFF
    qi = pl.program_id(1)
    scale = DIFF_QK_DIM ** -0.5
    lq = lam_ref[...]
    lam = (jnp.exp(jnp.sum(lq[0:1] * lq[1:2], axis=-1, keepdims=True))
           - jnp.exp(jnp.sum(lq[2:3] * lq[3:4], axis=-1, keepdims=True)) + LAMBDA_INIT)
    diag_bias = jnp.where(lax.broadcasted_iota(I32, (t, 1), 0) <= lax.broadcasted_iota(I32, (1, t), 1),
                          0.0, NEG)
    row = lax.broadcasted_iota(I32, (LANE, 1), 0)

    def tile_of(h):
        return (h * 2 * DIFF_QK_DIM) // LANE * LANE

    for h in range(DIFF_HEADS):
        q128 = qt_ref[0, tile_of(h):tile_of(h) + LANE, :]
        for mp in range(2):
            off = h * 2 * DIFF_QK_DIM + mp * DIFF_QK_DIM - tile_of(h)
            qm_ref[2 * h + mp] = jnp.where((row >= off) & (row < off + DIFF_QK_DIM), q128, jnp.zeros_like(q128))
    m_ref[...] = jnp.full(m_ref.shape, NEG, F32)
    l_ref[...] = jnp.zeros(l_ref.shape, F32)
    acc_ref[...] = jnp.zeros(acc_ref.shape, F32)

    for h0 in range(0, DIFF_HEADS, DIFF_HEADS_PER_PASS):
        def step(c, carry, bias=None):
            k0 = pl.multiple_of(c * t, t)
            chains = [(h, j) for h in range(h0, h0 + DIFF_HEADS_PER_PASS) for j in (2 * h, 2 * h + 1)]
            for h, j in chains:
                k = k_ref[0, pl.ds(k0, t), tile_of(h):tile_of(h) + LANE]
                s = _dot(k, qm_ref[j])
                s_ref[j - 2 * h0] = s if bias is None else s + bias
            for h, j in chains:
                s = s_ref[j - 2 * h0]
                m = m_ref[j:j + 1, :]
                m_new = jnp.maximum(m, jnp.max(s, axis=0, keepdims=True))
                alpha = jnp.exp((m - m_new) * scale)
                p = jnp.exp(s * scale - m_new * scale)
                l_ref[j:j + 1, :] = alpha * l_ref[j:j + 1, :] + jnp.sum(p, axis=0, keepdims=True)
                p_ref[j - 2 * h0] = p.astype(BF16)
                alpha_ref[j:j + 1, :] = alpha
                m_ref[j:j + 1, :] = m_new
            for h, j in chains:
                vt = vt_ref[0, h * DIFF_V_DIM:(h + 1) * DIFF_V_DIM, pl.ds(k0, t)]
                acc_ref[j] = alpha_ref[j:j + 1, :] * acc_ref[j] + _dot(vt, p_ref[j - 2 * h0])
            return carry
        lax.fori_loop(0, qi, step, 0)
        step(qi, 0, diag_bias)

    for h in range(DIFF_HEADS):
        o = (acc_ref[2 * h] / l_ref[2 * h:2 * h + 1, :]
             - lam * (acc_ref[2 * h + 1] / l_ref[2 * h + 1:2 * h + 2, :]))
        o = o * lax.rsqrt(jnp.mean(o * o, axis=0, keepdims=True) + EPS) * go_ref[...] * (1.0 - LAMBDA_INIT)
        ot_ref[0, h * DIFF_V_DIM:(h + 1) * DIFF_V_DIM, :] = o.astype(BF16)


def _diff(dq, dkv, lam4, g_out):
    B, S, W = dq.shape
    t = TQ_DIFF
    qt = dq.transpose(0, 2, 1)
    vt = dkv[:, :, W:].transpose(0, 2, 1)
    ot = pl.pallas_call(
        _diff_kernel,
        grid=(B, S // t),
        in_specs=[pl.BlockSpec((1, W, t), lambda b, q: (b, 0, q)),
                  pl.BlockSpec((1, S, W), lambda b, q: (b, 0, 0)),
                  pl.BlockSpec((1, W, S), lambda b, q: (b, 0, 0)),
                  pl.BlockSpec(lam4.shape, lambda b, q: (0, 0)),
                  pl.BlockSpec((DIFF_V_DIM, 1), lambda b, q: (0, 0))],
        out_specs=pl.BlockSpec((1, W, t), lambda b, q: (b, 0, q)),
        out_shape=jax.ShapeDtypeStruct((B, W, S), BF16),
        scratch_shapes=[pltpu.VMEM((2 * DIFF_HEADS, LANE, t), BF16),
                        pltpu.VMEM((2 * DIFF_HEADS, t), F32), pltpu.VMEM((2 * DIFF_HEADS, t), F32),
                        pltpu.VMEM((2 * DIFF_HEADS, t), F32),
                        pltpu.VMEM((2 * DIFF_HEADS, DIFF_V_DIM, t), F32),
                        pltpu.VMEM((2 * DIFF_HEADS_PER_PASS, t, t), F32),
                        pltpu.VMEM((2 * DIFF_HEADS_PER_PASS, t, t), BF16)],
        compiler_params=_params("arbitrary", "arbitrary"),
    )(qt, dkv, vt, lam4, g_out.reshape(DIFF_V_DIM, 1))
    return ot.transpose(0, 2, 1)


def _outproj_kernel(x_ref, on_ref, od_ref, wo_ref, mod_ref, gffn_ref, wr_ref, br_ref,
                    x1_ref, h_ref, route_ref):
    half = on_ref.shape[2]
    gate1 = mod_ref[0, 2:3, :]
    shift2, scale2 = mod_ref[0, 3:4, :], mod_ref[0, 4:5, :]
    attn = _dot(on_ref[0], wo_ref[0:half, :]) + _dot(od_ref[0], wo_ref[half:2 * half, :])
    x1 = x_ref[0] + gate1 * attn
    x1_ref[0] = x1
    ms = jnp.mean(x1 * x1, axis=-1, keepdims=True)
    h = x1 * lax.rsqrt(ms + EPS) * gffn_ref[...] * (1.0 + scale2) + shift2
    h_ref[0] = h

    r = _dot(h, wr_ref[...], precision=HIGHEST) + br_ref[...]
    lg, le = r[:, :LANE], r[:, LANE:]
    lane = lax.broadcasted_iota(I32, (1, LANE), 1).astype(F32)
    lg = jnp.where(lane < N_EXPERT_GROUPS, lg, SENTINEL)
    mg = jnp.max(lg, axis=-1, keepdims=True)
    p_grp = 1.0 / jnp.sum(jnp.exp(lg - mg), axis=-1, keepdims=True)
    grp = jnp.min(jnp.where(lg == mg, lane, float(LANE)), axis=-1, keepdims=True)
    lo = grp * EXPERTS_PER_GROUP
    le = jnp.where((lane >= lo) & (lane < lo + EXPERTS_PER_GROUP), le, SENTINEL)
    v1 = jnp.max(le, axis=-1, keepdims=True)
    i1 = jnp.min(jnp.where(le == v1, lane, float(LANE)), axis=-1, keepdims=True)
    le = jnp.where(lane == i1, SENTINEL, le)
    v2 = jnp.max(le, axis=-1, keepdims=True)
    i2 = jnp.min(jnp.where(le == v2, lane, float(LANE)), axis=-1, keepdims=True)
    e2 = jnp.exp(v2 - v1)
    w1 = p_grp / (1.0 + e2)
    w2 = p_grp * e2 / (1.0 + e2)
    route_ref[0] = jnp.where(lane == 0, i1, jnp.where(lane == 1, i2,
                             jnp.where(lane == 2, w1, jnp.where(lane == 3, w2, 0.0))))


def _outproj(x, o_nsa, o_diff, w_out, mod3, g_ffn, w_rg, b_rg, w_re, b_re):
    B, S, D = x.shape
    tm = TM_PROJ
    half = o_nsa.shape[2]
    wr = jnp.zeros((D, 2 * LANE), F32).at[:, :N_EXPERT_GROUPS].set(w_rg)
    wr = wr.at[:, LANE:LANE + N_EXPERTS].set(w_re)
    br = jnp.zeros((1, 2 * LANE), F32).at[0, :N_EXPERT_GROUPS].set(b_rg)
    br = br.at[0, LANE:LANE + N_EXPERTS].set(b_re)

    def tok(width):
        return pl.BlockSpec((1, tm, width), lambda b, s: (b, s, 0))

    return pl.pallas_call(
        _outproj_kernel,
        grid=(B, S // tm),
        in_specs=[tok(D), tok(half), tok(half),
                  pl.BlockSpec((D, D), lambda b, s: (0, 0)),
                  pl.BlockSpec((1, 6, D), lambda b, s: (b, 0, 0)),
                  pl.BlockSpec((1, D), lambda b, s: (0, 0)),
                  pl.BlockSpec((D, 2 * LANE), lambda b, s: (0, 0)),
                  pl.BlockSpec((1, 2 * LANE), lambda b, s: (0, 0))],
        out_specs=[tok(D), tok(D), tok(LANE)],
        out_shape=[jax.ShapeDtypeStruct((B, S, D), F32), jax.ShapeDtypeStruct((B, S, D), F32),
                   jax.ShapeDtypeStruct((B, S, LANE), F32)],
        compiler_params=_params("arbitrary", "arbitrary"),
    )(x, o_nsa, o_diff, w_out.astype(BF16), mod3, g_ffn.reshape(1, D), wr, br)


def _row_copy(src_hbm, dst, row, slot, sem):
    return pltpu.make_async_copy(src_hbm.at[pl.ds(row, 1)], dst.at[pl.ds(slot, 1)], sem)


def _prefetch_rows(i, n_steps, n_rows, idx_hbm, idx_smem, idx_sem, src_hbm, rows, row_sem):
    slot = i % 2

    def idx_copy(step, s):
        return pltpu.make_async_copy(idx_hbm.at[step], idx_smem.at[s], idx_sem.at[s])

    def issue_rows(s):
        def body(r, carry):
            _row_copy(src_hbm, rows.at[s], idx_smem[s, r // LANE, r % LANE], r, row_sem.at[s]).start()
            return carry
        lax.fori_loop(0, n_rows, body, 0, unroll=8)

    @pl.when(i == 0)
    def _():
        idx_copy(0, 0).start()
        idx_copy(0, 0).wait()
        issue_rows(0)

        @pl.when(n_steps > 1)
        def _():
            idx_copy(1, 1).start()

    @pl.when(i + 1 < n_steps)
    def _():
        idx_copy(i + 1, 1 - slot).wait()
        issue_rows(1 - slot)

        @pl.when(i + 2 < n_steps)
        def _():
            idx_copy(i + 2, slot).start()


def _wait_rows(n_rows, src_hbm, dst, sem):
    pltpu.make_async_copy(src_hbm.at[pl.ds(0, n_rows)], dst, sem).wait()


def _expert_kernel(blk_e_ref, n_used_ref, idx_hbm, h_hbm, wg_ref, wu_ref, wd_ref, o_ref,
                   idx_smem, xbuf, idx_sem, row_sem):
    i = pl.program_id(0)
    n_used = n_used_ref[0]
    slot = i % 2
    _prefetch_rows(i, n_used, MOE_BLOCK, idx_hbm, idx_smem, idx_sem, h_hbm, xbuf, row_sem)

    @pl.when(i < n_used)
    def _():
        _wait_rows(MOE_BLOCK, h_hbm, xbuf.at[slot], row_sem.at[slot])
        xb = xbuf[slot].astype(BF16)
        a = _dot(xb, wg_ref[0])
        u = _dot(xb, wu_ref[0])
        act = a * (1.0 / (1.0 + jnp.exp(-a))) * u
        o_ref[...] = _dot(act.astype(BF16), wd_ref[0])

    @pl.when(i >= n_used)
    def _():
        o_ref[...] = jnp.zeros_like(o_ref)


def _experts(blk_e, n_used, row_src, h, w_g, w_u, w_d):
    bm = MOE_BLOCK
    n_blk = row_src.shape[0]
    D = h.shape[1]
    H = w_g.shape[2]
    grid_spec = pltpu.PrefetchScalarGridSpec(
        num_scalar_prefetch=2,
        grid=(n_blk,),
        in_specs=[pl.BlockSpec(memory_space=pl.ANY), pl.BlockSpec(memory_space=pl.ANY),
                  pl.BlockSpec((1, D, H), lambda i, be, nu: (be[i], 0, 0)),
                  pl.BlockSpec((1, D, H), lambda i, be, nu: (be[i], 0, 0)),
                  pl.BlockSpec((1, H, D), lambda i, be, nu: (be[i], 0, 0))],
        out_specs=pl.BlockSpec((bm, D), lambda i, be, nu: (i, 0)),
        scratch_shapes=[pltpu.SMEM((2, SUBLANE, LANE), I32), pltpu.VMEM((2, bm, D), h.dtype),
                        pltpu.SemaphoreType.DMA((2,)), pltpu.SemaphoreType.DMA((2,))])
    return pl.pallas_call(
        _expert_kernel,
        grid_spec=grid_spec,
        out_shape=jax.ShapeDtypeStruct((n_blk * bm, D), F32),
        compiler_params=_params("arbitrary"),
    )(blk_e, n_used, row_src, h, w_g.astype(BF16), w_u.astype(BF16), w_d.astype(BF16))


def _combine_kernel(pos_hbm, yb_hbm, x1_ref, route_ref, mod_ref, o_ref, idx_smem, rows_ref, idx_sem, row_sem):
    i = pl.program_id(0)
    slot = i % 2
    _prefetch_rows(i, pl.num_programs(0), DMA_ROWS, pos_hbm, idx_smem, idx_sem, yb_hbm, rows_ref, row_sem)
    _wait_rows(DMA_ROWS, yb_hbm, rows_ref.at[slot], row_sem.at[slot])
    tm = DMA_ROWS // 2
    gate2 = mod_ref[0, 5:6, :]
    y = rows_ref[slot, 0:tm, :] * route_ref[:, 2:3] + rows_ref[slot, tm:2 * tm, :] * route_ref[:, 3:4]
    o_ref[...] = x1_ref[...] + gate2 * y


def _combine(pos, yb, x1, route, mod3, seq):
    T, D = x1.shape
    tm = DMA_ROWS // 2
    per_seq = seq // tm
    return pl.pallas_call(
        _combine_kernel,
        grid=(T // tm,),
        in_specs=[pl.BlockSpec(memory_space=pl.ANY), pl.BlockSpec(memory_space=pl.ANY),
                  pl.BlockSpec((tm, D), lambda i: (i, 0)),
                  pl.BlockSpec((tm, LANE), lambda i: (i, 0)),
                  pl.BlockSpec((1, 6, D), lambda i: (i // per_seq, 0, 0))],
        out_specs=pl.BlockSpec((tm, D), lambda i: (i, 0)),
        out_shape=jax.ShapeDtypeStruct((T, D), F32),
        scratch_shapes=[pltpu.SMEM((2, SUBLANE, LANE), I32), pltpu.VMEM((2, DMA_ROWS, D), F32),
                        pltpu.SemaphoreType.DMA((2,)), pltpu.SemaphoreType.DMA((2,))],
        compiler_params=_params("arbitrary"),
    )(pos, yb, x1, route, mod3)


def _dispatch_plan(route, T):
    bm = MOE_BLOCK
    e_flat = route[:, :2].astype(I32).reshape(-1)
    onehot = (e_flat[:, None] == jnp.arange(N_EXPERTS, dtype=I32)[None, :]).astype(I32)
    csum = jnp.cumsum(onehot, axis=0)
    rank = jnp.take_along_axis(csum, e_flat[:, None], axis=1)[:, 0] - 1
    counts = csum[-1]
    padded = (counts + bm - 1) // bm * bm
    pend = jnp.cumsum(padded)
    dest = (pend - padded)[e_flat] + rank
    n_blk = (2 * T + bm - 1) // bm + N_EXPERTS
    n_blk = (n_blk * bm + DMA_ROWS - 1) // DMA_ROWS * DMA_ROWS // bm
    row_src = jnp.zeros((n_blk * bm,), I32).at[dest].set(jnp.arange(2 * T, dtype=I32) // 2)
    blk_start = jnp.arange(n_blk, dtype=I32) * bm
    blk_e = jnp.minimum(jnp.sum((pend[None, :] <= blk_start[:, None]).astype(I32), axis=1), N_EXPERTS - 1)
    n_used = (pend[-1:] // bm).astype(I32)
    return dest, row_src, blk_e, n_used


def kernel(x, c, w_ada, b_ada, g_norm_mix, g_norm_ffn, w_in, g_nsa_q, g_nsa_k, pe_cmp, w_cmp1, w_cmp2, g_diff_q, g_diff_k, lam_q1, lam_k1, lam_q2, lam_k2, g_diff_out, w_out, w_router_group, b_router_group, w_router_expert, b_router_expert, w_exp_gate, w_exp_up, w_exp_down):
    B, S, D = x.shape
    T = B * S
    assert w_ada.shape[0] == 1, "single-layer operation"
    assert S % TK_NSA == 0 and S % TQ_DIFF == 0 and S % TM_PROJ == 0 and S % (DMA_ROWS // 2) == 0
    assert TK_NSA % TQ_ATT == 0 and S // CMP_STRIDE == LANE

    mod3 = _ada(c, w_ada[0], b_ada[0]).reshape(B, 6, D)
    qn, qr, kcv, kv, gates, dq, dkv = _inproj(
        x, mod3, g_norm_mix[0], w_in[0], g_nsa_q[0], g_nsa_k[0], g_diff_q[0], g_diff_k[0])
    kcmp, vcmp = _compress(kcv, pe_cmp[0], w_cmp1[0], w_cmp2[0], g_nsa_k[0, 0])
    o_nsa = _nsa_t(qn, qr, kcmp, vcmp, kv, gates)
    lam4 = jnp.stack([lam_q1[0], lam_k1[0], lam_q2[0], lam_k2[0]])
    o_diff = _diff(dq, dkv, lam4, g_diff_out[0])
    x1, h2, route = _outproj(x, o_nsa, o_diff, w_out[0], mod3, g_norm_ffn[0],
                             w_router_group[0], b_router_group[0], w_router_expert[0], b_router_expert[0])

    route = route.reshape(T, LANE)
    dest, row_src, blk_e, n_used = _dispatch_plan(route, T)
    idx_rows = MOE_BLOCK // LANE
    row_src = jnp.pad(row_src.reshape(-1, idx_rows, LANE), ((0, 0), (0, SUBLANE - idx_rows), (0, 0)))
    yb = _experts(blk_e, n_used, row_src, h2.reshape(T, D), w_exp_gate[0], w_exp_up[0], w_exp_down[0])
    tm = DMA_ROWS // 2
    pos = dest.reshape(T // tm, tm, 2).transpose(0, 2, 1).reshape(T // tm, SUBLANE, LANE)
    out = _combine(pos, yb, x1.reshape(T, D), route, mod3, S)
    return out.reshape(B, S, D)
```
